```python
import math
import jax
import jax.numpy as jnp
from jax import lax
import numpy as np

D_MODEL = 1024
BATCH = 4
SEQ = 4096
DEPTH = 4

GRID_W = 64
CTX_LEN = 256
N_MIXERS = 3
Q_BLOCK = 128
ROPE_THETA = 10000.0
EPS = 1e-6
NEG_INF = -1e30
CONV_WIDTH = 3
N_MOD = 6

A_HEADS = 8
A_KV_HEADS = 2
A_HEAD_DIM = D_MODEL // A_HEADS

FILTER_BANDS = 16
FILTER_EMB_DIM = 1 + 2 * FILTER_BANDS
FILTER_WIDTH = 64
FILTER_SIN_FREQ = 1.0
DECAY_TARGET = 1e-2
DECAY_FAST_PCT = 0.3
DECAY_SLOW_PCT = 1.5

C_HEADS = 16
C_KV_HEADS = 2
C_HEAD_DIM = D_MODEL // C_HEADS
WINDOW = 128

D_FF = 2816

N_A = (DEPTH + 2) // 3
N_B = (DEPTH + 1) // 3
N_C = DEPTH // 3

kernel_name = "hybrid_interleaved_dit_ctx_prefix"


def _rmsnorm(x, g):
    xf = x.astype(jnp.float32)
    y = xf * lax.rsqrt(jnp.mean(xf * xf, axis=-1, keepdims=True) + EPS)
    return (y * g.astype(jnp.float32)).astype(x.dtype)


def _modulate(x, shift, scale):
    return x * (1.0 + scale) + shift


def _dwconv(u, w, b):
    half = CONV_WIDTH // 2
    l = u.shape[1]
    up = jnp.pad(u, ((0, 0), (half, half), (0, 0)))
    out = b
    for j in range(CONV_WIDTH):
        out = out + up[:, j:j + l] * w[j]
    return out


def _axial_rope(seq_len, head_dim):
    rows = seq_len // GRID_W
    row = jnp.repeat(jnp.arange(rows), GRID_W).astype(jnp.float32)
    col = jnp.tile(jnp.arange(GRID_W), rows).astype(jnp.float32)
    axis_dim = head_dim // 2
    inv = jnp.power(ROPE_THETA, -jnp.arange(0, axis_dim, 2, dtype=jnp.float32) / axis_dim)
    ang = jnp.concatenate([row[:, None] * inv[None], col[:, None] * inv[None]], axis=-1)
    return jnp.cos(ang), jnp.sin(ang)


def _rope(x, cos, sin):
    half = x.shape[-1] // 2
    shape = (cos.shape[0],) + (1,) * (x.ndim - 3) + (half,)
    cos = cos.reshape(shape).astype(x.dtype)
    sin = sin.reshape(shape).astype(x.dtype)
    x1, x2 = x[..., :half], x[..., half:]
    return jnp.concatenate([x1 * cos - x2 * sin, x1 * sin + x2 * cos], axis=-1)


def _project(h, w, bias, n_heads, n_kv, head_dim, with_q):
    q_cols = n_heads * head_dim
    if not with_q:
        w = w[:, q_cols:]
        bias = None if bias is None else bias[q_cols:]
    out = h @ w
    if bias is not None:
        out = out + bias
    bsz, l, _ = h.shape
    q = None
    if with_q:
        q = out[..., :q_cols].reshape(bsz, l, n_kv, n_heads // n_kv, head_dim)
        out = out[..., q_cols:]
    k, v = jnp.split(out, 2, axis=-1)
    return q, k.reshape(bsz, l, n_kv, head_dim), v.reshape(bsz, l, n_kv, head_dim)


def _dense_block_attention(q, k, v, scale):
    b, l, nkv, g, dh = q.shape
    nb = l // Q_BLOCK
    qb = jnp.moveaxis(q.reshape(b, nb, Q_BLOCK, nkv, g, dh), 1, 0)

    def one_block(qi):
        s = jnp.einsum('bqkgd,bskd->bkgqs', qi, k).astype(jnp.float32) * scale
        p = jax.nn.softmax(s, axis=-1).astype(v.dtype)
        return jnp.einsum('bkgqs,bskd->bqkgd', p, v)

    ob = lax.map(one_block, qb)
    return jnp.moveaxis(ob, 0, 1).reshape(b, l, nkv, g, dh)


def _sink_attention(q, k, v, sink_logit, scale):
    s = jnp.einsum('bqkgd,bskd->bkgqs', q, k).astype(jnp.float32) * scale
    s0 = jnp.broadcast_to(sink_logit[None, :, :, None, None], s.shape[:-1] + (1,))
    p = jax.nn.softmax(jnp.concatenate([s0, s], axis=-1), axis=-1)[..., 1:].astype(v.dtype)
    return jnp.einsum('bkgqs,bskd->bqkgd', p, v)


def _window_attention(q, k, v, kc, vc, sink_logit, scale):
    b, l, nkv, g, dh = q.shape
    nb = l // Q_BLOCK
    n_ctx = kc.shape[1]

    def band(t):
        tp = jnp.pad(t, ((0, 0), (Q_BLOCK, Q_BLOCK), (0, 0), (0, 0))).reshape(b, nb + 2, Q_BLOCK, nkv, dh)
        return jnp.concatenate([tp[:, :-2], tp[:, 1:-1], tp[:, 2:]], axis=2)

    kb, vb = band(k), band(v)
    qb = q.reshape(b, nb, Q_BLOCK, nkv, g, dh)
    offs = jnp.arange(3 * Q_BLOCK)[None, :] - Q_BLOCK
    rel = offs - jnp.arange(Q_BLOCK)[:, None]
    key_pos = jnp.arange(nb)[:, None] * Q_BLOCK + offs
    valid = (jnp.abs(rel) <= WINDOW)[None] & ((key_pos >= 0) & (key_pos < l))[:, None, :]

    def one_block(args):
        qi, ki, vi, mi = args
        s_loc = jnp.einsum('bqkgd,bskd->bkgqs', qi, ki).astype(jnp.float32) * scale
        s_loc = jnp.where(mi, s_loc, NEG_INF)
        s_ctx = jnp.einsum('bqkgd,bskd->bkgqs', qi, kc).astype(jnp.float32) * scale
        s0 = jnp.broadcast_to(sink_logit[None, :, :, None, None], s_ctx.shape[:-1] + (1,))
        p = jax.nn.softmax(jnp.concatenate([s0, s_ctx, s_loc], axis=-1), axis=-1).astype(v.dtype)
        return (jnp.einsum('bkgqs,bskd->bqkgd', p[..., 1:1 + n_ctx], vc)
                + jnp.einsum('bkgqs,bskd->bqkgd', p[..., 1 + n_ctx:], vi))

    ob = lax.map(one_block, (jnp.moveaxis(qb, 1, 0), jnp.moveaxis(kb, 1, 0), jnp.moveaxis(vb, 1, 0), valid))
    return jnp.moveaxis(ob, 0, 1).reshape(b, l, nkv, g, dh)


def _mixer_a(h_lat, h_ctx, w_qkv, q_gain, k_gain, w_out, cos, sin, need_ctx):
    scale = A_HEAD_DIM ** -0.5
    b, l, _ = h_lat.shape
    q, k, v = _project(h_lat, w_qkv, None, A_HEADS, A_KV_HEADS, A_HEAD_DIM, True)
    q = _rope(_rmsnorm(q, q_gain), cos, sin)
    k = _rope(_rmsnorm(k, k_gain), cos, sin)
    qc, kc, vc = _project(h_ctx, w_qkv, None, A_HEADS, A_KV_HEADS, A_HEAD_DIM, need_ctx)
    kc = _rmsnorm(kc, k_gain)
    k_all = jnp.concatenate([kc, k], axis=1)
    v_all = jnp.concatenate([vc, v], axis=1)
    o_lat = _dense_block_attention(q, k_all, v_all, scale).reshape(b, l, A_HEADS * A_HEAD_DIM) @ w_out
    o_ctx = None
    if need_ctx:
        oc = _dense_block_attention(_rmsnorm(qc, q_gain), kc, vc, scale)
        o_ctx = oc.reshape(b, h_ctx.shape[1], A_HEADS * A_HEAD_DIM) @ w_out
    return o_lat, o_ctx


def _hyena_filter(seq_len, w1, b1, w2, b2, w3, b3, w4):
    t = jnp.linspace(0.0, 1.0, seq_len, dtype=jnp.float32)[:, None]
    omega = (2.0 * math.pi / seq_len) * jnp.arange(seq_len, dtype=jnp.float32)
    bands = jnp.linspace(1e-4, FILTER_BANDS - 1, FILTER_BANDS, dtype=jnp.float32)
    ang = omega[:, None] * bands[None, :]
    z = jnp.concatenate([t, jnp.cos(ang), -jnp.sin(ang)], axis=-1)
    f = jnp.sin(FILTER_SIN_FREQ * (z @ w1 + b1))
    f = jnp.sin(FILTER_SIN_FREQ * (f @ w2 + b2))
    f = jnp.sin(FILTER_SIN_FREQ * (f @ w3 + b3))
    f = (f @ w4).astype(jnp.float32)
    rates = jnp.abs(jnp.linspace(math.log(DECAY_TARGET) / DECAY_FAST_PCT,
                                 math.log(DECAY_TARGET) / DECAY_SLOW_PCT, D_MODEL, dtype=jnp.float32))
    decay = jnp.exp(-t * rates[None, :])
    return f[:, :D_MODEL] * decay, f[:, D_MODEL:] * decay


def _bidir_long_conv(u, h_fwd, h_bwd, skip):
    l = u.shape[1]
    n = 2 * l
    filt = jnp.concatenate([h_fwd, jnp.zeros_like(h_fwd[:1]), h_bwd[:0:-1]], axis=0)
    uf = jnp.fft.rfft(u.astype(jnp.float32), n=n, axis=1)
    ff = jnp.fft.rfft(filt, n=n, axis=0)
    y = jnp.fft.irfft(uf * ff[None], n=n, axis=1)[:, :l]
    return (y + u.astype(jnp.float32) * skip.astype(jnp.float32)).astype(u.dtype)


def _hyena_mixer(h, w_in, b_in, conv_w, conv_b, f_w1, f_b1, f_w2, f_b2, f_w3, f_b3, f_w4, skip, w_out, b_out):
    u = _dwconv(h @ w_in + b_in, conv_w, conv_b)
    x0, x1, v = jnp.split(u, 3, axis=-1)
    h_fwd, h_bwd = _hyena_filter(h.shape[1], f_w1, f_b1, f_w2, f_b2, f_w3, f_b3, f_w4)
    v = _bidir_long_conv(v * x1, h_fwd, h_bwd, skip)
    return (v * x0) @ w_out + b_out


def _mixer_c(h_lat, h_ctx, w_qkv, b_qkv, sink, w_out, b_out, cos, sin, need_ctx):
    scale = C_HEAD_DIM ** -0.5
    b, l, _ = h_lat.shape
    q, k, v = _project(h_lat, w_qkv, b_qkv, C_HEADS, C_KV_HEADS, C_HEAD_DIM, True)
    q, k = _rope(q, cos, sin), _rope(k, cos, sin)
    qc, kc, vc = _project(h_ctx, w_qkv, b_qkv, C_HEADS, C_KV_HEADS, C_HEAD_DIM, need_ctx)
    sink_logit = sink.reshape(C_KV_HEADS, C_HEADS // C_KV_HEADS).astype(jnp.float32)
    o = _window_attention(q, k, v, kc, vc, sink_logit, scale)
    o_lat = o.reshape(b, l, C_HEADS * C_HEAD_DIM) @ w_out + b_out
    o_ctx = None
    if need_ctx:
        oc = _sink_attention(qc, kc, vc, sink_logit, scale)
        o_ctx = oc.reshape(b, h_ctx.shape[1], C_HEADS * C_HEAD_DIM) @ w_out + b_out
    return o_lat, o_ctx


def _conv_ffn(h, w_up, conv_w, conv_b, w_down):
    gate, up = jnp.split(h @ w_up, 2, axis=-1)
    return (jax.nn.silu(_dwconv(gate, conv_w, conv_b)) * up) @ w_down


def setup_inputs(seed: int = 0) -> dict:
    key = jax.random.key(seed)
    ks = iter(jax.random.split(key, 40))

    def nrm(shape, scale=1.0):
        return jax.random.normal(next(ks), shape, jnp.float32) * scale

    d = D_MODEL
    a_cols = (A_HEADS + 2 * A_KV_HEADS) * A_HEAD_DIM
    c_cols = (C_HEADS + 2 * C_KV_HEADS) * C_HEAD_DIM
    return {
        'x': nrm((BATCH, SEQ, d)),
        'c': nrm((BATCH, d)),
        'ctx': nrm((BATCH, CTX_LEN, d)),
        'c_ctx': nrm((d,)),
        'mod_w': nrm((DEPTH, d, N_MOD * d), d ** -0.5),
        'mod_b': nrm((DEPTH, N_MOD * d), 0.02),
        'norm_g': 1.0 + nrm((DEPTH, 4, d), 0.05),
        'ffn_w_up': nrm((DEPTH, d, 2 * D_FF), d ** -0.5),
        'ffn_conv_w': nrm((DEPTH, CONV_WIDTH, D_FF), CONV_WIDTH ** -0.5),
        'ffn_conv_b': nrm((DEPTH, D_FF), 0.02),
        'ffn_w_down': nrm((DEPTH, D_FF, d), D_FF ** -0.5),
        'a_w_qkv': nrm((N_A, d, a_cols), d ** -0.5),
        'a_q_gain': 1.0 + nrm((N_A, A_HEAD_DIM), 0.05),
        'a_k_gain': 1.0 + nrm((N_A, A_HEAD_DIM), 0.05),
        'a_w_out': nrm((N_A, A_HEADS * A_HEAD_DIM, d), (A_HEADS * A_HEAD_DIM) ** -0.5),
        'b_w_in': nrm((N_B, d, 3 * d), d ** -0.5),
        'b_b_in': nrm((N_B, 3 * d), 0.02),
        'b_conv_w': nrm((N_B, CONV_WIDTH, 3 * d), CONV_WIDTH ** -0.5),
        'b_conv_b': nrm((N_B, 3 * d), 0.02),
        'b_f_w1': nrm((N_B, FILTER_EMB_DIM, FILTER_WIDTH), FILTER_EMB_DIM ** -0.5),
        'b_f_b1': nrm((N_B, FILTER_WIDTH), 0.1),
        'b_f_w2': nrm((N_B, FILTER_WIDTH, FILTER_WIDTH), FILTER_WIDTH ** -0.5),
        'b_f_b2': nrm((N_B, FILTER_WIDTH), 0.1),
        'b_f_w3': nrm((N_B, FILTER_WIDTH, FILTER_WIDTH), FILTER_WIDTH ** -0.5),
        'b_f_b3': nrm((N_B, FILTER_WIDTH), 0.1),
        'b_f_w4': nrm((N_B, FILTER_WIDTH, 2 * d), FILTER_WIDTH ** -0.5),
        'b_skip': nrm((N_B, d)),
        'b_w_out': nrm((N_B, d, d), d ** -0.5),
        'b_b_out': nrm((N_B, d), 0.02),
        'c_w_qkv': nrm((N_C, d, c_cols), d ** -0.5),
        'c_b_qkv': nrm((N_C, c_cols), 0.02),
        'c_sink': nrm((N_C, C_HEADS)),
        'c_w_out': nrm((N_C, C_HEADS * C_HEAD_DIM, d), (C_HEADS * C_HEAD_DIM) ** -0.5),
        'c_b_out': nrm((N_C, d), 0.02),
    }


def reference(x, c, ctx, c_ctx, mod_w, mod_b, norm_g, ffn_w_up, ffn_conv_w, ffn_conv_b, ffn_w_down,
              a_w_qkv, a_q_gain, a_k_gain, a_w_out,
              b_w_in, b_b_in, b_conv_w, b_conv_b, b_f_w1, b_f_b1, b_f_w2, b_f_b2, b_f_w3, b_f_b3, b_f_w4,
              b_skip, b_w_out, b_b_out,
              c_w_qkv, c_b_qkv, c_sink, c_w_out, c_b_out):
    seq_len = x.shape[1]
    cos_a, sin_a = _axial_rope(seq_len, A_HEAD_DIM)
    cos_c, sin_c = _axial_rope(seq_len, C_HEAD_DIM)
    s_lat = jax.nn.silu(c)
    s_ctx = jax.nn.silu(c_ctx)
    h_lat, h_ctx = x, ctx
    for i in range(DEPTH):
        need_ctx = i < DEPTH - 1
        kind, j = i % N_MIXERS, i // N_MIXERS
        g = norm_g[i]
        m_lat = jnp.split((s_lat @ mod_w[i] + mod_b[i])[:, None, :], N_MOD, axis=-1)
        m_ctx = jnp.split(s_ctx @ mod_w[i] + mod_b[i], N_MOD, axis=-1)
        u_lat = _modulate(_rmsnorm(h_lat, g[0]), m_lat[0], m_lat[1])
        if kind == 0:
            u_ctx = _modulate(_rmsnorm(h_ctx, g[0]), m_ctx[0], m_ctx[1])
            o_lat, o_ctx = _mixer_a(u_lat, u_ctx, a_w_qkv[j], a_q_gain[j], a_k_gain[j], a_w_out[j],
                                    cos_a, sin_a, need_ctx)
        elif kind == 1:
            hy = (b_w_in[j], b_b_in[j], b_conv_w[j], b_conv_b[j], b_f_w1[j], b_f_b1[j], b_f_w2[j], b_f_b2[j],
                  b_f_w3[j], b_f_b3[j], b_f_w4[j], b_skip[j], b_w_out[j], b_b_out[j])
            o_lat = _hyena_mixer(u_lat, *hy)
            o_ctx = None
            if need_ctx:
                u_ctx = _modulate(_rmsnorm(h_ctx, g[0]), m_ctx[0], m_ctx[1])
                o_ctx = _hyena_mixer(u_ctx, *hy)
        else:
            u_ctx = _modulate(_rmsnorm(h_ctx, g[0]), m_ctx[0], m_ctx[1])
            o_lat, o_ctx = _mixer_c(u_lat, u_ctx, c_w_qkv[j], c_b_qkv[j], c_sink[j], c_w_out[j], c_b_out[j],
                                    cos_c, sin_c, need_ctx)
        ffn = (ffn_w_up[i], ffn_conv_w[i], ffn_conv_b[i], ffn_w_down[i])
        h_lat = h_lat + m_lat[2] * _rmsnorm(o_lat, g[1])
        f_lat = _conv_ffn(_modulate(_rmsnorm(h_lat, g[2]), m_lat[3], m_lat[4]), *ffn)
        h_lat = h_lat + m_lat[5] * _rmsnorm(f_lat, g[3])
        if need_ctx:
            h_ctx = h_ctx + m_ctx[2] * _rmsnorm(o_ctx, g[1])
            f_ctx = _conv_ffn(_modulate(_rmsnorm(h_ctx, g[2]), m_ctx[3], m_ctx[4]), *ffn)
            h_ctx = h_ctx + m_ctx[5] * _rmsnorm(f_ctx, g[3])
    return h_lat
```

```python
import functools
import math

import numpy as np
import jax
import jax.numpy as jnp
from jax import lax
from jax.experimental import pallas as pl
from jax.experimental.pallas import tpu as pltpu

F32 = jnp.float32
BF16 = jnp.bfloat16

D = 1024
B = 4
L = 4096
CTX = 256
DEPTH = 4
N_MOD = 6
EPS = 1e-6
NEG_INF = -1e30
LOG2_E = math.log2(math.e)
GRID_W = 64
ROPE_THETA = 10000.0

A_HEADS, A_KV, A_DH = 8, 2, 128
C_HEADS, C_KV, C_DH = 16, 2, 64
QKV_COLS = 1536
D_FF = 2816
FILTER_BANDS = 16
FILTER_EMB = 1 + 2 * FILTER_BANDS
FILTER_WIDTH = 64
FILTER_SIN_FREQ = 1.0
DECAY_TARGET = 1e-2
DECAY_FAST_PCT = 0.3
DECAY_SLOW_PCT = 1.5

T_LAT = B * L
T_CTX = B * CTX
T = T_LAT + T_CTX

TM = 512
HALO = 16
N_LAT_TILES = T_LAT // TM
N_TILES = T // TM
FC = 1408
N_FC = D_FF // FC
TQ = 128
TK = 2048
NQ_LAT = L // TQ
NQ_CTX = CTX // TQ

FFT_N = 2 * L
FFT_N1 = 128
FFT_N2 = FFT_N // FFT_N1
FFT_H1 = FFT_N1 // 2
FFT_K1 = FFT_H1 + 1
FFT_K1P = 72
FFT_GT = 12
MM_TN = 8192

VMEM_MB = 1024 * 1024


def _cparams(sem, vmem_mb):
    return pltpu.CompilerParams(dimension_semantics=sem, vmem_limit_bytes=vmem_mb * VMEM_MB)


def _rope_tables(head_dim):
    rows = L // GRID_W
    row = np.repeat(np.arange(rows), GRID_W).astype(np.float64)
    col = np.tile(np.arange(GRID_W), rows).astype(np.float64)
    axis_dim = head_dim // 2
    inv = np.power(ROPE_THETA, -np.arange(0, axis_dim, 2, dtype=np.float64) / axis_dim)
    ang = np.concatenate([row[:, None] * inv[None], col[:, None] * inv[None]], axis=-1)
    cos, sin = np.cos(ang), np.sin(ang)
    reps = 128 // head_dim
    cos_t = np.tile(np.concatenate([cos, cos], axis=-1), (1, reps))
    sin_t = np.tile(np.concatenate([-sin, sin], axis=-1), (1, reps))
    cos_t = np.concatenate([cos_t, np.ones((TM, 128))], axis=0)
    sin_t = np.concatenate([sin_t, np.zeros((TM, 128))], axis=0)
    return jnp.asarray(cos_t, F32), jnp.asarray(sin_t, F32)


def _two_stage_dft():
    n1s, n2s, h1, k1n, k1p, n = FFT_N1, FFT_N2, FFT_H1, FFT_K1, FFT_K1P, FFT_N
    n1 = np.arange(h1)
    k1 = np.arange(k1n)
    ang = 2 * np.pi * np.outer(k1, n1) / n1s
    m1 = np.zeros((2 * k1p, h1))
    m1[0:2 * k1n:2] = np.cos(ang)
    m1[1:2 * k1n:2] = -np.sin(ang)
    c = np.where((k1 == 0) | (k1 == h1), 1.0, 2.0)
    m1i = np.zeros((h1, 2 * k1p))
    m1i[:, 0:2 * k1n:2] = (c[None] / n) * np.cos(ang.T)
    m1i[:, 1:2 * k1n:2] = -(c[None] / n) * np.sin(ang.T)
    n2 = np.arange(n2s)
    k2 = np.arange(n2s)
    tf = np.zeros((k1p, 2 * n2s, 2 * n2s))
    ti = np.zeros((k1p, 2 * n2s, 2 * n2s))
    for a in range(k1n):
        th = 2 * np.pi * np.outer(a + n1s * k2, n2) / n
        er, ei = np.cos(th), -np.sin(th)
        tf[a] = np.block([[er, -ei], [ei, er]])
        er, ei = np.cos(th.T), np.sin(th.T)
        ti[a] = np.block([[er, -ei], [ei, er]])
    return tuple(jnp.asarray(a, F32).astype(BF16) for a in (m1, m1i, tf, ti))


def _direct_dft(seq_len):
    n = 2 * seq_len
    th = 2 * np.pi * np.outer(np.arange(n), np.arange(seq_len)) / n
    fwd = np.concatenate([np.cos(th), -np.sin(th)], axis=0)[None]
    inv = np.concatenate([np.cos(th.T), -np.sin(th.T)], axis=1)[None] / n
    return jnp.asarray(fwd, F32).astype(BF16), jnp.asarray(inv, F32).astype(BF16)


def _filter_features(seq_len):
    t = np.linspace(0.0, 1.0, seq_len)[:, None]
    omega = (2.0 * math.pi / seq_len) * np.arange(seq_len, dtype=np.float64)
    bands = np.linspace(1e-4, FILTER_BANDS - 1, FILTER_BANDS)
    ang = omega[:, None] * bands[None, :]
    z = np.concatenate([t, np.cos(ang), -np.sin(ang)], axis=-1)
    z = np.pad(z, ((0, 0), (0, FILTER_WIDTH - FILTER_EMB)))
    rates = np.abs(np.linspace(math.log(DECAY_TARGET) / DECAY_FAST_PCT,
                               math.log(DECAY_TARGET) / DECAY_SLOW_PCT, D))[None]
    return jnp.asarray(z, F32), jnp.asarray(t, F32), jnp.asarray(rates, F32)


def _rms(x, g):
    return x * lax.rsqrt(jnp.mean(x * x, axis=-1, keepdims=True) + EPS) * g


def _norm_mod(h, g, shift, scale):
    return _rms(h, g) * (1.0 + scale) + shift


def _silu(x):
    return x * (1.0 / (1.0 + jnp.exp(-x)))


def _seg_masks(i):
    rows = i * TM + lax.broadcasted_iota(jnp.int32, (TM, 1), 0)
    seg = jnp.where(i < N_LAT_TILES, L, CTX)
    pos = rows & (seg - 1)
    return (pos != 0).astype(F32), (pos != seg - 1).astype(F32)


def _fill_xn(xn_ref, h_ref, hp_ref, hn_ref, g, shift, scale):
    xn_ref[0:HALO, :] = _norm_mod(hp_ref[...], g, shift, scale).astype(BF16)
    xn_ref[HALO:HALO + TM, :] = _norm_mod(h_ref[...], g, shift, scale).astype(BF16)
    xn_ref[HALO + TM:, :] = _norm_mod(hn_ref[...], g, shift, scale).astype(BF16)


def _conv3(u_ref, cw, cb, has_prev, has_next):
    return (cb + cw[0:1] * (u_ref[HALO - 1:HALO - 1 + TM, :] * has_prev)
            + cw[1:2] * u_ref[HALO:HALO + TM, :]
            + cw[2:3] * (u_ref[HALO + 1:HALO + 1 + TM, :] * has_next))


def _mod_row(i):
    return jnp.where(i < N_LAT_TILES, i // (L // TM), B)


def _row_specs(in_rows):
    last = in_rows // HALO - 1
    per = TM // HALO
    return [
        pl.BlockSpec((TM, D), lambda i, *_: (i, 0)),
        pl.BlockSpec((HALO, D), lambda i, *_: (jnp.maximum(i * per - 1, 0), 0)),
        pl.BlockSpec((HALO, D), lambda i, *_: (jnp.minimum((i + 1) * per, last), 0)),
    ]


def _mods_kernel(s_ref, w_ref, b_ref, o_ref):
    s = _silu(s_ref[...])
    o_ref[...] = jnp.dot(s, w_ref[...], preferred_element_type=F32,
                         precision=lax.Precision.HIGHEST) + b_ref[...]


def _mods(cs, mod_w, mod_b):
    tn = 1536
    return pl.pallas_call(
        _mods_kernel,
        grid=(DEPTH, N_MOD * D // tn),
        in_specs=[
            pl.BlockSpec((8, D), lambda l, n: (0, 0)),
            pl.BlockSpec((None, D, tn), lambda l, n: (l, 0, n)),
            pl.BlockSpec((None, 1, tn), lambda l, n: (l, 0, n)),
        ],
        out_specs=pl.BlockSpec((None, 8, tn), lambda l, n: (l, 0, n)),
        out_shape=jax.ShapeDtypeStruct((DEPTH, 8, N_MOD * D), F32),
        compiler_params=_cparams(("parallel", "parallel"), 40),
        name="mods",
    )(cs, mod_w, mod_b.reshape(DEPTH, 1, N_MOD * D))


def _proj_attn_kernel(h_ref, mod_ref, g_ref, w_ref, b_ref, qg_ref, kg_ref, cos_ref, sin_ref, o_ref, *, kind):
    m = mod_ref[...]
    xn = _norm_mod(h_ref[...], g_ref[0:1, :], m[:, 0:D], m[:, D:2 * D]).astype(BF16)
    cos, sin = cos_ref[...], sin_ref[...]
    n_rope = 10
    if kind == "c":
        lane = lax.broadcasted_iota(jnp.int32, (TM, 128), 1)
        first_half = (lane & (C_DH - 1)) < C_DH // 2
    for j in range(QKV_COLS // 128):
        cols = slice(j * 128, (j + 1) * 128)
        y = jnp.dot(xn, w_ref[:, cols], preferred_element_type=F32) + b_ref[:, cols]
        if j < n_rope:
            if kind == "a":
                gain = qg_ref[...] if j < A_HEADS else kg_ref[...]
                y = _rms(y, gain)
                rot = pltpu.roll(y, 64, 1)
                scale = A_DH ** -0.5 * LOG2_E if j < A_HEADS else 1.0
            else:
                rot = jnp.where(first_half, pltpu.roll(y, 128 - C_DH // 2, 1), pltpu.roll(y, C_DH // 2, 1))
                scale = C_DH ** -0.5 * LOG2_E if j < 8 else 1.0
            y = (y * cos + rot * sin) * scale
        o_ref[:, cols] = y.astype(BF16)


def _proj_attn(h, mods_l, g, w, b, q_gain, k_gain, cos_t, sin_t, kind):
    tpb = L // TM
    rope_idx = lambda i: (jnp.where(i < N_LAT_TILES, i % tpb, tpb), 0)
    return pl.pallas_call(
        functools.partial(_proj_attn_kernel, kind=kind),
        grid=(N_TILES,),
        in_specs=[
            pl.BlockSpec((TM, D), lambda i: (i, 0)),
            pl.BlockSpec((None, 1, N_MOD * D), lambda i: (_mod_row(i), 0, 0)),
            pl.BlockSpec((4, D), lambda i: (0, 0)),
            pl.BlockSpec((D, QKV_COLS), lambda i: (0, 0)),
            pl.BlockSpec((1, QKV_COLS), lambda i: (0, 0)),
            pl.BlockSpec((1, 128), lambda i: (0, 0)),
            pl.BlockSpec((1, 128), lambda i: (0, 0)),
            pl.BlockSpec((TM, 128), rope_idx),
            pl.BlockSpec((TM, 128), rope_idx),
        ],
        out_specs=pl.BlockSpec((TM, QKV_COLS), lambda i: (i, 0)),
        out_shape=jax.ShapeDtypeStruct((T, QKV_COLS), BF16),
        compiler_params=_cparams(("parallel",), 40),
        name="proj_" + kind,
    )(h, mods_l, g, w, b, q_gain, k_gain, cos_t, sin_t)


def _qk(q, k):
    return lax.dot_general(q, k, (((1,), (1,)), ((), ())), preferred_element_type=F32)


def _ones_ext(v):
    return jnp.concatenate([v, jnp.ones(v.shape, BF16)], axis=1)


def _attn_a_kernel(q_ref, *refs, latent):
    if latent:
        kl_ref, vl_ref, kc_ref, vc_ref, o_ref = refs
        chunks = [[(kc_ref[...], vc_ref[...]), (kl_ref[0:TK, :], vl_ref[0:TK, :])]]
        chunks += [[(kl_ref[c * TK:(c + 1) * TK, :], vl_ref[c * TK:(c + 1) * TK, :])] for c in range(1, L // TK)]
    else:
        kc_ref, vc_ref, o_ref = refs
        chunks = [[(kc_ref[...], vc_ref[...])]]
    g = A_HEADS // A_KV
    m = [None] * g
    acc = [None] * g
    for ci, pieces in enumerate(chunks):
        v_exts = [_ones_ext(v) for _, v in pieces]
        scores = [jnp.concatenate([_qk(q_ref[:, i * A_DH:(i + 1) * A_DH], k) for k, _ in pieces], axis=1)
                  for i in range(g)]
        for i in range(g):
            s = scores[i]
            m_cur = jnp.broadcast_to(jnp.max(s, axis=-1, keepdims=True), (TQ, 128))
            m_new = m_cur if ci == 0 else jnp.maximum(m[i], m_cur)
            p = jnp.exp2(s - jnp.tile(m_new, (1, s.shape[1] // 128))).astype(BF16)
            pv, off = None, 0
            for (k, _), v_ext in zip(pieces, v_exts):
                part = jnp.dot(p[:, off:off + k.shape[0]], v_ext, preferred_element_type=F32)
                pv = part if pv is None else pv + part
                off += k.shape[0]
            if ci == 0:
                acc[i] = pv
            else:
                acc[i] = jnp.tile(jnp.exp2(m[i] - m_new), (1, 2)) * acc[i] + pv
            m[i] = m_new
    for i in range(g):
        o_ref[:, i * A_DH:(i + 1) * A_DH] = (acc[i][:, 0:A_DH] / acc[i][:, A_DH:]).astype(BF16)


def _attn_a(qkv, latent):
    nq = NQ_LAT if latent else NQ_CTX
    q0 = 0 if latent else B * NQ_LAT
    gcols = (A_HEADS // A_KV) * A_DH // 128
    in_specs = [pl.BlockSpec((TQ, gcols * 128), lambda b, kv, j: (q0 + b * nq + j, kv))]
    if latent:
        in_specs += [
            pl.BlockSpec((L, A_DH), lambda b, kv, j: (b, A_HEADS + kv)),
            pl.BlockSpec((L, A_DH), lambda b, kv, j: (b, A_HEADS + A_KV + kv)),
        ]
    in_specs += [
        pl.BlockSpec((CTX, A_DH), lambda b, kv, j: (T_LAT // CTX + b, A_HEADS + kv)),
        pl.BlockSpec((CTX, A_DH), lambda b, kv, j: (T_LAT // CTX + b, A_HEADS + A_KV + kv)),
    ]
    return pl.pallas_call(
        functools.partial(_attn_a_kernel, latent=latent),
        grid=(B, A_KV, nq),
        in_specs=in_specs,
        out_specs=pl.BlockSpec((TQ, gcols * 128), lambda b, kv, j: (b * nq + j, kv)),
        out_shape=jax.ShapeDtypeStruct((B * nq * TQ, D), BF16),
        compiler_params=_cparams(("parallel", "parallel", "arbitrary"), 48),
        name="attn_a_lat" if latent else "attn_a_ctx",
    )(*([qkv] * len(in_specs)))


def _attn_c_kernel(q_ref, kp_ref, kc_ref, kn_ref, vp_ref, vc_ref, vn_ref, kx_ref, vx_ref, sink_ref, o_ref):
    j = pl.program_id(2)
    g = C_HEADS // C_KV
    lane = lax.broadcasted_iota(jnp.int32, (TQ, 128), 1)
    lo = lane < C_DH
    zero = jnp.zeros((TQ, 128), BF16)
    k_all = jnp.concatenate([kx_ref[...], kp_ref[...], kc_ref[...], kn_ref[...]], axis=0)
    v_ext = _ones_ext(jnp.concatenate([vx_ref[...], vp_ref[...], vc_ref[...], vn_ref[...]], axis=0))
    r = lax.broadcasted_iota(jnp.int32, (TQ, TQ), 0)
    far = 4 * TQ
    is_lat = j < NQ_LAT
    off_p = jnp.where(jnp.logical_and(is_lat, j > 0), 0, far)
    off_c = jnp.where(is_lat, 0, far)
    off_n = jnp.where(j < NQ_LAT - 1, 0, far)
    bias = jnp.concatenate([
        jnp.zeros((TQ, CTX), F32),
        jnp.where(lane >= r + off_p, 0.0, NEG_INF),
        jnp.where(lane >= off_c, 0.0, NEG_INF),
        jnp.where(lane <= r - off_n, 0.0, NEG_INF)], axis=1)
    heads = []
    for p in range(g // 2):
        qb = q_ref[:, p * 128:(p + 1) * 128]
        heads += [jnp.where(lo, qb, zero), jnp.where(lo, zero, qb)]
    scores = [_qk(qh, k_all) + bias for qh in heads]
    outs = []
    for h in range(g):
        s = scores[h]
        sink = jnp.tile(sink_ref[h * 8:(h + 1) * 8, :], (TQ // 8, 1))
        m_part = sink
        for c in range(s.shape[1] // 128):
            m_part = jnp.maximum(m_part, s[:, c * 128:(c + 1) * 128])
        m = jnp.broadcast_to(jnp.max(m_part, axis=-1, keepdims=True), (TQ, 128))
        p = jnp.exp2(s - jnp.tile(m, (1, s.shape[1] // 128))).astype(BF16)
        pv = jnp.dot(p, v_ext, preferred_element_type=F32)
        outs.append(pv[:, 0:128] / (pv[:, 128:] + jnp.exp2(sink - m)))
    for p in range(g // 2):
        o_ref[:, p * 128:(p + 1) * 128] = jnp.where(lo, outs[2 * p], outs[2 * p + 1]).astype(BF16)


def _attn_c(qkv, sink_rows):
    nq = NQ_LAT + NQ_CTX
    qcols = C_HEADS * C_DH // C_KV
    kcol = C_HEADS * C_DH // 128
    vcol = kcol + C_KV
    n_lat_blocks = T_LAT // TQ

    def q_idx(b, kv, j):
        return (jnp.where(j < NQ_LAT, b * NQ_LAT + j, n_lat_blocks + b * NQ_CTX + (j - NQ_LAT)), kv)

    def loc_idx(delta, col):
        def idx(b, kv, j):
            blk = b * NQ_LAT + jnp.minimum(j, NQ_LAT - 1) + delta
            return (jnp.clip(blk, 0, n_lat_blocks - 1), col + kv)
        return idx

    loc = lambda delta, col: pl.BlockSpec((TQ, 128), loc_idx(delta, col))
    ctx = lambda col: pl.BlockSpec((CTX, 128), lambda b, kv, j: (T_LAT // CTX + b, col + kv))
    return pl.pallas_call(
        _attn_c_kernel,
        grid=(B, C_KV, nq),
        in_specs=[
            pl.BlockSpec((TQ, qcols), q_idx),
            loc(-1, kcol), loc(0, kcol), loc(1, kcol),
            loc(-1, vcol), loc(0, vcol), loc(1, vcol),
            ctx(kcol), ctx(vcol),
            pl.BlockSpec((None, (C_HEADS // C_KV) * 8, 128), lambda b, kv, j: (kv, 0, 0)),
        ],
        out_specs=pl.BlockSpec((TQ, qcols), q_idx),
        out_shape=jax.ShapeDtypeStruct((T, D), BF16),
        compiler_params=_cparams(("parallel", "parallel", "arbitrary"), 40),
        name="attn_c",
    )(qkv, qkv, qkv, qkv, qkv, qkv, qkv, qkv, qkv, sink_rows)


def _proj_b_kernel(h_ref, hp_ref, hn_ref, mod_ref, g_ref, w_ref, b_ref, cw_ref, cb_ref,
                   x0_ref, z_ref, xn_ref, u_ref):
    i = pl.program_id(0)
    m = mod_ref[...]
    _fill_xn(xn_ref, h_ref, hp_ref, hn_ref, g_ref[0:1, :], m[:, 0:D], m[:, D:2 * D])
    has_prev, has_next = _seg_masks(i)

    def conv_group(k):
        cols = slice(k * D, (k + 1) * D)
        u_ref[...] = jnp.dot(xn_ref[...], w_ref[:, cols], preferred_element_type=F32) + b_ref[:, cols]
        return _conv3(u_ref, cw_ref[:, cols], cb_ref[:, cols], has_prev, has_next)

    x0_ref[...] = conv_group(0).astype(BF16)
    x1 = conv_group(1)
    v = conv_group(2)
    z_ref[...] = (v * x1).astype(BF16)


def _proj_b(h, mods_l, g, w, b, cw, cb):
    full = lambda shape: pl.BlockSpec(shape, lambda i: (0,) * len(shape))
    return pl.pallas_call(
        _proj_b_kernel,
        grid=(N_TILES,),
        in_specs=_row_specs(T) + [
            pl.BlockSpec((None, 1, N_MOD * D), lambda i: (_mod_row(i), 0, 0)),
            full((4, D)), full((D, 3 * D)), full((1, 3 * D)), full((3, 3 * D)), full((1, 3 * D)),
        ],
        out_specs=[pl.BlockSpec((TM, D), lambda i: (i, 0))] * 2,
        out_shape=[jax.ShapeDtypeStruct((T, D), BF16)] * 2,
        scratch_shapes=[pltpu.VMEM((TM + 2 * HALO, D), BF16), pltpu.VMEM((TM + 2 * HALO, D), F32)],
        compiler_params=_cparams(("parallel",), 48),
        name="proj_b",
    )(h, h, h, mods_l, g, w, b, cw, cb)


def _filter_kernel(zf_ref, t_ref, w1_ref, b1_ref, w2_ref, b2_ref, w3_ref, b3_ref, w4_ref, rates_ref, o_ref):
    i = pl.program_id(0)
    tm = zf_ref.shape[0]
    dot = functools.partial(jnp.dot, preferred_element_type=F32, precision=lax.Precision.HIGHEST)
    f = jnp.sin(FILTER_SIN_FREQ * (dot(zf_ref[...], w1_ref[...]) + b1_ref[...]))
    f = jnp.sin(FILTER_SIN_FREQ * (dot(f, w2_ref[...]) + b2_ref[...]))
    f = jnp.sin(FILTER_SIN_FREQ * (dot(f, w3_ref[...]) + b3_ref[...]))
    f = dot(f, w4_ref[...])
    decay = jnp.exp(-t_ref[...] * rates_ref[...])
    row = i * tm + lax.broadcasted_iota(jnp.int32, (tm, 1), 0)
    o_ref[:, 0:D] = (f[:, 0:D] * decay).astype(BF16)
    o_ref[:, D:2 * D] = (f[:, D:2 * D] * decay * (row != 0).astype(F32)).astype(BF16)


def _hyena_filter(seq_len, w1, b1, w2, b2, w3, b3, w4):
    zf, t, rates = _filter_features(seq_len)
    tm = min(seq_len, 512)
    w1p = jnp.pad(w1, ((0, FILTER_WIDTH - FILTER_EMB), (0, 0)))
    full = lambda shape: pl.BlockSpec(shape, lambda i: (0,) * len(shape))
    fw = FILTER_WIDTH
    return pl.pallas_call(
        _filter_kernel,
        grid=(seq_len // tm,),
        in_specs=[
            pl.BlockSpec((tm, fw), lambda i: (i, 0)),
            pl.BlockSpec((tm, 1), lambda i: (i, 0)),
            full((fw, fw)), full((1, fw)), full((fw, fw)), full((1, fw)), full((fw, fw)), full((1, fw)),
            full((fw, 2 * D)), full((1, D)),
        ],
        out_specs=pl.BlockSpec((tm, 2 * D), lambda i: (i, 0)),
        out_shape=jax.ShapeDtypeStruct((seq_len, 2 * D), BF16),
        compiler_params=_cparams(("parallel",), 40),
        name="hyena_filter",
    )(zf, t, w1p, b1.reshape(1, fw), w2, b2.reshape(1, fw), w3, b3.reshape(1, fw), w4, rates)


def _mm_kernel(a_ref, x_ref, o_ref):
    o_ref[...] = jnp.dot(a_ref[...], x_ref[...], preferred_element_type=F32).astype(o_ref.dtype)


def _const_mm(a, x2d, n_batch, out_dtype):
    m, k = a.shape
    n = x2d.shape[1]
    return pl.pallas_call(
        _mm_kernel,
        grid=(n_batch, n // MM_TN),
        in_specs=[pl.BlockSpec((m, k), lambda b, t: (0, 0)), pl.BlockSpec((k, MM_TN), lambda b, t: (b, t))],
        out_specs=pl.BlockSpec((m, MM_TN), lambda b, t: (b, t)),
        out_shape=jax.ShapeDtypeStruct((n_batch * m, n), out_dtype),
        compiler_params=_cparams(("parallel", "parallel"), 40),
        name="dft_stage1",
    )(a, x2d)


def _group_mm_kernel(t_ref, d_ref, *rest, mode, gt):
    if mode == "product":
        hf_ref, o_ref = rest
    else:
        (o_ref,) = rest
    half = t_ref.shape[1] // 2

    def body(g, carry):
        x = jnp.dot(t_ref[g], d_ref[g], preferred_element_type=F32)
        if mode == "none":
            o_ref[g] = x.astype(o_ref.dtype)
        elif mode == "combine":
            o_ref[g, 0:half, :] = (x[0:half, 0:D] + x[0:half, D:2 * D]).astype(o_ref.dtype)
            o_ref[g, half:, :] = (x[half:, 0:D] - x[half:, D:2 * D]).astype(o_ref.dtype)
        else:
            hr = hf_ref[g, 0:half, :].astype(F32)
            hi = hf_ref[g, half:, :].astype(F32)
            xr, xi = x[0:half], x[half:]
            o_ref[g, 0:half, :] = (xr * hr - xi * hi).astype(o_ref.dtype)
            o_ref[g, half:, :] = (xr * hi + xi * hr).astype(o_ref.dtype)
        return carry

    lax.fori_loop(0, gt, body, 0)


def _group_mm(tmat, data, n_batch, b_off, mode, out_dtype, hf=None):
    n_g, r, k = tmat.shape
    c = data.shape[3]
    gt = FFT_GT if n_g % FFT_GT == 0 else n_g
    c_out = D if mode == "combine" else c
    in_specs = [
        pl.BlockSpec((gt, r, k), lambda gi, b: (gi, 0, 0)),
        pl.BlockSpec((None, gt, k, c), lambda gi, b: (b + b_off, gi, 0, 0)),
    ]
    args = [tmat, data]
    if mode == "product":
        in_specs.append(pl.BlockSpec((gt, r, c), lambda gi, b: (gi, 0, 0)))
        args.append(hf)
    return pl.pallas_call(
        functools.partial(_group_mm_kernel, mode=mode, gt=gt),
        grid=(n_g // gt, n_batch),
        in_specs=in_specs,
        out_specs=pl.BlockSpec((None, gt, r, c_out), lambda gi, b: (b, gi, 0, 0)),
        out_shape=jax.ShapeDtypeStruct((n_batch, n_g, r, c_out), out_dtype),
        compiler_params=_cparams(("parallel", "arbitrary"), 48),
        name="dft_stage2_" + mode,
    )(*args)


def _hyena_long_conv(z, filt_lat, filt_ctx):
    m1, m1i, tf, ti = _two_stage_dft()
    rows2 = 2 * FFT_K1P
    lanes = FFT_N2 * D
    af = _const_mm(m1, filt_lat.reshape(FFT_H1, FFT_N2 * 2 * D), 1, BF16)
    hf = _group_mm(tf, af.reshape(1, FFT_K1P, 2 * FFT_N2, 2 * D), 1, 0, "combine", BF16)[0]
    a = _const_mm(m1, z.reshape(T * D // lanes, lanes), B, BF16)
    p = _group_mm(tf, a.reshape(B, FFT_K1P, 2 * FFT_N2, D), B, 0, "product", BF16, hf)
    gd = _group_mm(ti, p, B, 0, "none", BF16)
    y_lat = _const_mm(m1i, gd.reshape(B * rows2, lanes), B, F32).reshape(T_LAT, D)
    fc, fci = _direct_dft(CTX)
    hfc = _group_mm(fc, filt_ctx.reshape(1, 1, CTX, 2 * D), 1, 0, "combine", BF16)[0]
    pc = _group_mm(fc, z.reshape(T // CTX, 1, CTX, D), B, T_LAT // CTX, "product", BF16, hfc)
    y_ctx = _group_mm(fci, pc, B, 0, "none", F32).reshape(T_CTX, D)
    return y_lat, y_ctx


def _outproj_kernel(h_ref, mod_ref, g_ref, w_ref, b_ref, *rest, hyena, split):
    i = pl.program_id(0)
    if split:
        lat_ref, ctx_ref = rest[0:2]
        y = jnp.where(i < N_LAT_TILES, lat_ref[...], ctx_ref[...])
        rest = rest[2:]
    else:
        y = rest[0][...]
        rest = rest[1:]
    if hyena:
        z_ref, x0_ref, skip_ref, o_ref = rest
        y = ((y + z_ref[...].astype(F32) * skip_ref[...]) * x0_ref[...].astype(F32)).astype(BF16)
    else:
        (o_ref,) = rest
    m = mod_ref[...]
    r = jnp.dot(y, w_ref[...], preferred_element_type=F32) + b_ref[...]
    o_ref[...] = h_ref[...] + m[:, 2 * D:3 * D] * _rms(r, g_ref[1:2, :])


def _outproj(h, mods_l, g, w, b, y_lat, y_ctx, n_tiles, hyena_args=None):
    tile = lambda i: (i, 0)
    in_specs = [
        pl.BlockSpec((TM, D), tile),
        pl.BlockSpec((None, 1, N_MOD * D), lambda i: (_mod_row(i), 0, 0)),
        pl.BlockSpec((4, D), lambda i: (0, 0)),
        pl.BlockSpec((D, D), lambda i: (0, 0)),
        pl.BlockSpec((1, D), lambda i: (0, 0)),
    ]
    split = y_ctx is not None
    if split:
        in_specs += [
            pl.BlockSpec((TM, D), lambda i: (jnp.minimum(i, N_LAT_TILES - 1), 0)),
            pl.BlockSpec((TM, D), lambda i: (jnp.maximum(i - N_LAT_TILES, 0), 0)),
        ]
        args = [y_lat, y_ctx]
    else:
        in_specs.append(pl.BlockSpec((TM, D), tile))
        args = [y_lat]
    hyena = hyena_args is not None
    if hyena:
        in_specs += [pl.BlockSpec((TM, D), tile), pl.BlockSpec((TM, D), tile), pl.BlockSpec((1, D), lambda i: (0, 0))]
        args += list(hyena_args)
    return pl.pallas_call(
        functools.partial(_outproj_kernel, hyena=hyena, split=split),
        grid=(n_tiles,),
        in_specs=in_specs,
        out_specs=pl.BlockSpec((TM, D), tile),
        out_shape=jax.ShapeDtypeStruct((n_tiles * TM, D), F32),
        compiler_params=_cparams(("parallel",), 40),
        name="outproj",
    )(h, mods_l, g, w, b, *args)


def _ffn_kernel(h_ref, hp_ref, hn_ref, mod_ref, g_ref, wg_ref, wu_ref, cw_ref, cb_ref, wd_ref, o_ref,
                xn_ref, gate_ref, acc_ref):
    i = pl.program_id(0)
    c = pl.program_id(1)
    m = mod_ref[...]

    @pl.when(c == 0)
    def _():
        _fill_xn(xn_ref, h_ref, hp_ref, hn_ref, g_ref[2:3, :], m[:, 3 * D:4 * D], m[:, 4 * D:5 * D])
        acc_ref[...] = jnp.zeros_like(acc_ref)

    gate_ref[...] = jnp.dot(xn_ref[...], wg_ref[...], preferred_element_type=F32)
    up = jnp.dot(xn_ref[HALO:HALO + TM, :], wu_ref[...], preferred_element_type=F32)
    has_prev, has_next = _seg_masks(i)
    act = _silu(_conv3(gate_ref, cw_ref[...], cb_ref[...], has_prev, has_next)) * up
    acc_ref[...] += jnp.dot(act.astype(BF16), wd_ref[...], preferred_element_type=F32)

    @pl.when(c == N_FC - 1)
    def _():
        o_ref[...] = h_ref[...] + m[:, 5 * D:6 * D] * _rms(acc_ref[...], g_ref[3:4, :])


def _ffn(h, mods_l, g, w_up, cw, cb, w_down, n_tiles):
    return pl.pallas_call(
        _ffn_kernel,
        grid=(n_tiles, N_FC),
        in_specs=_row_specs(h.shape[0]) + [
            pl.BlockSpec((None, 1, N_MOD * D), lambda i, c: (_mod_row(i), 0, 0)),
            pl.BlockSpec((4, D), lambda i, c: (0, 0)),
            pl.BlockSpec((D, FC), lambda i, c: (0, c)),
            pl.BlockSpec((D, FC), lambda i, c: (0, N_FC + c)),
            pl.BlockSpec((3, FC), lambda i, c: (0, c)),
            pl.BlockSpec((1, FC), lambda i, c: (0, c)),
            pl.BlockSpec((FC, D), lambda i, c: (c, 0)),
        ],
        out_specs=pl.BlockSpec((TM, D), lambda i, c: (i, 0)),
        out_shape=jax.ShapeDtypeStruct((n_tiles * TM, D), F32),
        scratch_shapes=[
            pltpu.VMEM((TM + 2 * HALO, D), BF16),
            pltpu.VMEM((TM + 2 * HALO, FC), F32),
            pltpu.VMEM((TM, D), F32),
        ],
        compiler_params=_cparams(("parallel", "arbitrary"), 56),
        name="ffn",
    )(h, h, h, mods_l, g, w_up, w_up, cw, cb, w_down)


def _dup_heads(w):
    k0, k1 = w[..., 0:C_DH], w[..., C_DH:2 * C_DH]
    return jnp.concatenate([k0, k0, k1, k1], axis=-1)


def kernel(x, c, ctx, c_ctx, mod_w, mod_b, norm_g, ffn_w_up, ffn_conv_w, ffn_conv_b, ffn_w_down, a_w_qkv, a_q_gain, a_k_gain, a_w_out, b_w_in, b_b_in, b_conv_w, b_conv_b, b_f_w1, b_f_b1, b_f_w2, b_f_b2, b_f_w3, b_f_b3, b_f_w4, b_skip, b_w_out, b_b_out, c_w_qkv, c_b_qkv, c_sink, c_w_out, c_b_out):
    h = jnp.concatenate([x.reshape(T_LAT, D), ctx.reshape(T_CTX, D)], axis=0)
    cs = jnp.concatenate([c, c_ctx[None], jnp.zeros((8 - B - 1, D), F32)], axis=0)
    mods = _mods(cs, mod_w, mod_b).reshape(DEPTH, 8, 1, N_MOD * D)
    cos_a, sin_a = _rope_tables(A_DH)
    cos_c, sin_c = _rope_tables(C_DH)
    zeros_d = jnp.zeros((1, D), F32)

    for i in range(DEPTH):
        last = i == DEPTH - 1
        kind, j = i % 3, i // 3
        g = norm_g[i]
        mods_l = mods[i]
        n_tiles = N_LAT_TILES if last else N_TILES
        if kind == 0:
            qkv = _proj_attn(h, mods_l, g, a_w_qkv[j].astype(BF16), jnp.zeros((1, QKV_COLS), F32),
                             a_q_gain[j].reshape(1, A_DH), a_k_gain[j].reshape(1, A_DH), cos_a, sin_a, "a")
            o_lat = _attn_a(qkv, latent=True)
            o_ctx = None if last else _attn_a(qkv, latent=False)
            h = _outproj(h, mods_l, g, a_w_out[j].astype(BF16), zeros_d, o_lat, o_ctx, n_tiles)
        elif kind == 1:
            x0, z = _proj_b(h, mods_l, g, b_w_in[j].astype(BF16), b_b_in[j].reshape(1, 3 * D),
                            b_conv_w[j], b_conv_b[j].reshape(1, 3 * D))
            fw = (b_f_w1[j], b_f_b1[j], b_f_w2[j], b_f_b2[j], b_f_w3[j], b_f_b3[j], b_f_w4[j])
            y_lat, y_ctx = _hyena_long_conv(z, _hyena_filter(L, *fw), _hyena_filter(CTX, *fw))
            h = _outproj(h, mods_l, g, b_w_out[j].astype(BF16), b_b_out[j].reshape(1, D), y_lat, y_ctx, n_tiles,
                         hyena_args=(z, x0, b_skip[j].reshape(1, D)))
        else:
            qc = C_HEADS * C_DH
            kc = qc + C_KV * C_DH
            w, bias = c_w_qkv[j], c_b_qkv[j].reshape(1, -1)
            w = jnp.concatenate([w[:, :qc], _dup_heads(w[:, qc:kc]), _dup_heads(w[:, kc:])], axis=1)
            bias = jnp.concatenate([bias[:, :qc], _dup_heads(bias[:, qc:kc]), _dup_heads(bias[:, kc:])], axis=1)
            ones = jnp.ones((1, 128), F32)
            qkv = _proj_attn(h, mods_l, g, w.astype(BF16), bias, ones, ones, cos_c, sin_c, "c")
            sink_rows = jnp.broadcast_to((c_sink[j].astype(F32) * LOG2_E)[:, None, None],
                                         (C_HEADS, 8, 128)).reshape(C_KV, (C_HEADS // C_KV) * 8, 128)
            o = _attn_c(qkv, sink_rows)
            h = _outproj(h, mods_l, g, c_w_out[j].astype(BF16), c_b_out[j].reshape(1, D), o, None, n_tiles)
        h = _ffn(h, mods_l, g, ffn_w_up[i].astype(BF16), ffn_conv_w[i], ffn_conv_b[i].reshape(1, D_FF),
                 ffn_w_down[i].astype(BF16), n_tiles)
    return h.reshape(B, L, D)
```

```python
import functools
import math

import numpy as np
import jax
import jax.numpy as jnp
from jax import lax
from jax.experimental import pallas as pl
from jax.experimental.pallas import tpu as pltpu

F32 = jnp.float32
BF16 = jnp.bfloat16

D = 1024
B = 4
L = 4096
CTX = 256
DEPTH = 4
N_MOD = 6
EPS = 1e-6
NEG_INF = -1e30
LOG2_E = math.log2(math.e)
GRID_W = 64
ROPE_THETA = 10000.0

A_HEADS, A_KV, A_DH = 8, 2, 128
C_HEADS, C_KV, C_DH = 16, 2, 64
QKV_COLS = 1536
PROJ_GROUP = 4
SUB_ROWS = 256
D_FF = 2816
FILTER_BANDS = 16
FILTER_EMB = 1 + 2 * FILTER_BANDS
FILTER_WIDTH = 64
FILTER_SIN_FREQ = 1.0
DECAY_TARGET = 1e-2
DECAY_FAST_PCT = 0.3
DECAY_SLOW_PCT = 1.5

T_LAT = B * L
T_CTX = B * CTX
T = T_LAT + T_CTX

TM = 512
HALO = 16
N_LAT_TILES = T_LAT // TM
N_TILES = T // TM
FC = 1408
N_FC = D_FF // FC
TQ = 128
TK = 2048
NQ_LAT = L // TQ
NQ_CTX = CTX // TQ

FFT_N = 2 * L
FFT_N1 = 128
FFT_N2 = FFT_N // FFT_N1
FFT_H1 = FFT_N1 // 2
FFT_K1 = FFT_H1 + 1
FFT_R = 144
FFT_CT = 256
FFT_SUB = 8
FFT_UNROLL = 13

VMEM_MB = 1024 * 1024


def _cparams(sem, vmem_mb):
    return pltpu.CompilerParams(dimension_semantics=sem, vmem_limit_bytes=vmem_mb * VMEM_MB)


def _rope_tables(head_dim):
    rows = L // GRID_W
    row = np.repeat(np.arange(rows), GRID_W).astype(np.float64)
    col = np.tile(np.arange(GRID_W), rows).astype(np.float64)
    axis_dim = head_dim // 2
    inv = np.power(ROPE_THETA, -np.arange(0, axis_dim, 2, dtype=np.float64) / axis_dim)
    ang = np.concatenate([row[:, None] * inv[None], col[:, None] * inv[None]], axis=-1)
    cos, sin = np.cos(ang), np.sin(ang)
    reps = 128 // head_dim
    cos_t = np.tile(np.concatenate([cos, cos], axis=-1), (1, reps))
    sin_t = np.tile(np.concatenate([-sin, sin], axis=-1), (1, reps))
    cos_t = np.concatenate([cos_t, np.ones((TM, 128))], axis=0)
    sin_t = np.concatenate([sin_t, np.zeros((TM, 128))], axis=0)
    return jnp.asarray(cos_t, F32), jnp.asarray(sin_t, F32)


def _two_stage_dft():
    n1s, n2s, h1, k1n, n = FFT_N1, FFT_N2, FFT_H1, FFT_K1, FFT_N
    n1 = np.arange(h1)
    k1 = np.arange(k1n)
    ang = 2 * np.pi * np.outer(k1, n1) / n1s
    m1 = np.zeros((FFT_R, h1))
    m1[0:2 * k1n:2] = np.cos(ang)
    m1[1:2 * k1n:2] = -np.sin(ang)
    c = np.where((k1 == 0) | (k1 == h1), 1.0, 2.0)
    m1i = np.zeros((h1, FFT_R))
    m1i[:, 0:2 * k1n:2] = (c[None] / n) * np.cos(ang.T)
    m1i[:, 1:2 * k1n:2] = -(c[None] / n) * np.sin(ang.T)
    n2 = np.arange(n2s)
    k2 = np.arange(n2s)
    tf = np.zeros((k1n, 2 * n2s, 2 * n2s))
    ti = np.zeros((k1n, 2 * n2s, 2 * n2s))
    for a in range(k1n):
        th = 2 * np.pi * np.outer(a + n1s * k2, n2) / n
        er, ei = np.cos(th), -np.sin(th)
        tf[a] = np.block([[er, -ei], [ei, er]])
        er, ei = np.cos(th.T), np.sin(th.T)
        ti[a] = np.block([[er, -ei], [ei, er]])
    eye = np.eye(FFT_SUB)
    return tuple(jnp.asarray(a, F32).astype(BF16) for a in (np.kron(m1, eye), np.kron(m1i, eye), tf, ti))


def _direct_dft(seq_len):
    n = 2 * seq_len
    th = 2 * np.pi * np.outer(np.arange(n), np.arange(seq_len)) / n
    fwd = np.concatenate([np.cos(th), -np.sin(th)], axis=0)[None]
    inv = np.concatenate([np.cos(th.T), -np.sin(th.T)], axis=1)[None] / n
    return jnp.asarray(fwd, F32).astype(BF16), jnp.asarray(inv, F32).astype(BF16)


def _filter_features(seq_len):
    t = np.linspace(0.0, 1.0, seq_len)[:, None]
    omega = (2.0 * math.pi / seq_len) * np.arange(seq_len, dtype=np.float64)
    bands = np.linspace(1e-4, FILTER_BANDS - 1, FILTER_BANDS)
    ang = omega[:, None] * bands[None, :]
    z = np.concatenate([t, np.cos(ang), -np.sin(ang)], axis=-1)
    z = np.pad(z, ((0, 0), (0, FILTER_WIDTH - FILTER_EMB)))
    rates = np.abs(np.linspace(math.log(DECAY_TARGET) / DECAY_FAST_PCT,
                               math.log(DECAY_TARGET) / DECAY_SLOW_PCT, D))[None]
    return jnp.asarray(z, F32), jnp.asarray(t, F32), jnp.asarray(rates, F32)


def _rms(x, g):
    return x * lax.rsqrt(jnp.mean(x * x, axis=-1, keepdims=True) + EPS) * g


def _norm_mod(h, g, shift, scale):
    return _rms(h, g) * (1.0 + scale) + shift


def _silu(x):
    return x * (1.0 / (1.0 + jnp.exp(-x)))


def _seg_masks(i):
    rows = i * TM + lax.broadcasted_iota(jnp.int32, (TM, 1), 0)
    seg = jnp.where(i < N_LAT_TILES, L, CTX)
    pos = rows & (seg - 1)
    return (pos != 0).astype(F32), (pos != seg - 1).astype(F32)


def _fill_xn(xn_ref, h_ref, hp_ref, hn_ref, g, shift, scale):
    xn_ref[0:HALO, :] = _norm_mod(hp_ref[...], g, shift, scale).astype(BF16)
    xn_ref[HALO:HALO + TM, :] = _norm_mod(h_ref[...], g, shift, scale).astype(BF16)
    xn_ref[HALO + TM:, :] = _norm_mod(hn_ref[...], g, shift, scale).astype(BF16)


def _conv3(u_ref, cw, cb, has_prev, has_next, n_rows):
    def piece(r0, r1, masked):
        prev = u_ref[HALO - 1 + r0:HALO - 1 + r1, :]
        nxt = u_ref[HALO + 1 + r0:HALO + 1 + r1, :]
        if masked:
            prev = prev * has_prev[r0:r1]
            nxt = nxt * has_next[r0:r1]
        return cb + cw[0:1] * prev + cw[1:2] * u_ref[HALO + r0:HALO + r1, :] + cw[2:3] * nxt

    pieces = []
    for s in range(0, n_rows, CTX):
        pieces += [piece(s, s + 8, True), piece(s + 8, s + CTX - 8, False), piece(s + CTX - 8, s + CTX, True)]
    return jnp.concatenate(pieces, axis=0)


def _mod_row(i):
    return jnp.where(i < N_LAT_TILES, i // (L // TM), B)


def _row_specs(in_rows):
    last = in_rows // HALO - 1
    per = TM // HALO
    return [
        pl.BlockSpec((TM, D), lambda i, *_: (i, 0)),
        pl.BlockSpec((HALO, D), lambda i, *_: (jnp.maximum(i * per - 1, 0), 0)),
        pl.BlockSpec((HALO, D), lambda i, *_: (jnp.minimum((i + 1) * per, last), 0)),
    ]


def _mods_kernel(s_ref, w_ref, b_ref, o_ref):
    s = _silu(s_ref[...])
    o_ref[...] = jnp.dot(s, w_ref[...], preferred_element_type=F32,
                         precision=lax.Precision.HIGHEST) + b_ref[...]


def _mods(cs, mod_w, mod_b):
    tn = 1536
    return pl.pallas_call(
        _mods_kernel,
        grid=(DEPTH, N_MOD * D // tn),
        in_specs=[
            pl.BlockSpec((8, D), lambda l, n: (0, 0)),
            pl.BlockSpec((None, D, tn), lambda l, n: (l, 0, n)),
            pl.BlockSpec((None, 1, tn), lambda l, n: (l, 0, n)),
        ],
        out_specs=pl.BlockSpec((None, 8, tn), lambda l, n: (l, 0, n)),
        out_shape=jax.ShapeDtypeStruct((DEPTH, 8, N_MOD * D), F32),
        compiler_params=_cparams(("parallel", "parallel"), 40),
        name="mods",
    )(cs, mod_w, mod_b.reshape(DEPTH, 1, N_MOD * D))


def _proj_attn_kernel(h_ref, mod_ref, g_ref, w_ref, b_ref, qg_ref, kg_ref, cos_ref, sin_ref, o_ref, *, kind):
    m = mod_ref[...]
    n_q = 8
    n_rope = 10
    q_scale = (A_DH if kind == "a" else C_DH) ** -0.5 * LOG2_E
    if kind == "c":
        lane = lax.broadcasted_iota(jnp.int32, (SUB_ROWS, 128), 1)
        first_half = (lane & (C_DH - 1)) < C_DH // 2
    for r0 in range(0, TM, SUB_ROWS):
        rows = slice(r0, r0 + SUB_ROWS)
        xn = _norm_mod(h_ref[rows, :], g_ref[0:1, :], m[:, 0:D], m[:, D:2 * D]).astype(BF16)
        cos, sin = cos_ref[rows, :], sin_ref[rows, :]
        cos_q, sin_q = cos * q_scale, sin * q_scale
        for j in range(QKV_COLS // 128):
            if j % PROJ_GROUP == 0:
                wide = slice(j * 128, (j + PROJ_GROUP) * 128)
                y_wide = jnp.dot(xn, w_ref[:, wide], preferred_element_type=F32) + b_ref[:, wide]
            y = y_wide[:, (j % PROJ_GROUP) * 128:(j % PROJ_GROUP + 1) * 128]
            if j < n_rope:
                if kind == "a":
                    y = _rms(y, qg_ref[...] if j < n_q else kg_ref[...])
                    rot = pltpu.roll(y, 64, 1)
                else:
                    rot = jnp.where(first_half, pltpu.roll(y, 128 - C_DH // 2, 1), pltpu.roll(y, C_DH // 2, 1))
                y = y * cos_q + rot * sin_q if j < n_q else y * cos + rot * sin
            o_ref[rows, j * 128:(j + 1) * 128] = y.astype(BF16)


def _proj_attn(h, mods_l, g, w, b, q_gain, k_gain, cos_t, sin_t, kind):
    tpb = L // TM
    rope_idx = lambda i: (jnp.where(i < N_LAT_TILES, i % tpb, tpb), 0)
    return pl.pallas_call(
        functools.partial(_proj_attn_kernel, kind=kind),
        grid=(N_TILES,),
        in_specs=[
            pl.BlockSpec((TM, D), lambda i: (i, 0)),
            pl.BlockSpec((None, 1, N_MOD * D), lambda i: (_mod_row(i), 0, 0)),
            pl.BlockSpec((4, D), lambda i: (0, 0)),
            pl.BlockSpec((D, QKV_COLS), lambda i: (0, 0)),
            pl.BlockSpec((1, QKV_COLS), lambda i: (0, 0)),
            pl.BlockSpec((1, 128), lambda i: (0, 0)),
            pl.BlockSpec((1, 128), lambda i: (0, 0)),
            pl.BlockSpec((TM, 128), rope_idx),
            pl.BlockSpec((TM, 128), rope_idx),
        ],
        out_specs=pl.BlockSpec((TM, QKV_COLS), lambda i: (i, 0)),
        out_shape=jax.ShapeDtypeStruct((T, QKV_COLS), BF16),
        compiler_params=_cparams(("parallel",), 40),
        name="proj_" + kind,
    )(h, mods_l, g, w, b, q_gain, k_gain, cos_t, sin_t)


def _qk(q, k):
    return lax.dot_general(q, k, (((1,), (1,)), ((), ())), preferred_element_type=F32)


def _ones_ext(v):
    return jnp.concatenate([v, jnp.ones(v.shape, BF16)], axis=1)


def _attn_a_kernel(q_ref, *refs, latent):
    if latent:
        kl_ref, vl_ref, kc_ref, vc_ref, o_ref = refs
        chunks = [[(kc_ref[...], vc_ref[...]), (kl_ref[0:TK, :], vl_ref[0:TK, :])]]
        chunks += [[(kl_ref[c * TK:(c + 1) * TK, :], vl_ref[c * TK:(c + 1) * TK, :])] for c in range(1, L // TK)]
    else:
        kc_ref, vc_ref, o_ref = refs
        chunks = [[(kc_ref[...], vc_ref[...])]]
    g = A_HEADS // A_KV
    m = [None] * g
    acc = [None] * g
    for ci, pieces in enumerate(chunks):
        v_exts = [_ones_ext(v) for _, v in pieces]
        scores = [jnp.concatenate([_qk(q_ref[:, i * A_DH:(i + 1) * A_DH], k) for k, _ in pieces], axis=1)
                  for i in range(g)]
        for i in range(g):
            s = scores[i]
            m_cur = jnp.broadcast_to(jnp.max(s, axis=-1, keepdims=True), (TQ, 128))
            m_new = m_cur if ci == 0 else jnp.maximum(m[i], m_cur)
            p = jnp.exp2(s - jnp.tile(m_new, (1, s.shape[1] // 128))).astype(BF16)
            pv, off = None, 0
            for (k, _), v_ext in zip(pieces, v_exts):
                part = jnp.dot(p[:, off:off + k.shape[0]], v_ext, preferred_element_type=F32)
                pv = part if pv is None else pv + part
                off += k.shape[0]
            if ci == 0:
                acc[i] = pv
            else:
                acc[i] = jnp.tile(jnp.exp2(m[i] - m_new), (1, 2)) * acc[i] + pv
            m[i] = m_new
    for i in range(g):
        o_ref[:, i * A_DH:(i + 1) * A_DH] = (acc[i][:, 0:A_DH] / acc[i][:, A_DH:]).astype(BF16)


def _attn_a(qkv, latent):
    nq = NQ_LAT if latent else NQ_CTX
    q0 = 0 if latent else B * NQ_LAT
    gcols = (A_HEADS // A_KV) * A_DH // 128
    in_specs = [pl.BlockSpec((TQ, gcols * 128), lambda b, kv, j: (q0 + b * nq + j, kv))]
    if latent:
        in_specs += [
            pl.BlockSpec((L, A_DH), lambda b, kv, j: (b, A_HEADS + kv)),
            pl.BlockSpec((L, A_DH), lambda b, kv, j: (b, A_HEADS + A_KV + kv)),
        ]
    in_specs += [
        pl.BlockSpec((CTX, A_DH), lambda b, kv, j: (T_LAT // CTX + b, A_HEADS + kv)),
        pl.BlockSpec((CTX, A_DH), lambda b, kv, j: (T_LAT // CTX + b, A_HEADS + A_KV + kv)),
    ]
    return pl.pallas_call(
        functools.partial(_attn_a_kernel, latent=latent),
        grid=(B, A_KV, nq),
        in_specs=in_specs,
        out_specs=pl.BlockSpec((TQ, gcols * 128), lambda b, kv, j: (b * nq + j, kv)),
        out_shape=jax.ShapeDtypeStruct((B * nq * TQ, D), BF16),
        compiler_params=_cparams(("parallel", "parallel", "arbitrary"), 48),
        name="attn_a_lat" if latent else "attn_a_ctx",
    )(*([qkv] * len(in_specs)))


def _attn_c_kernel(q_ref, kp_ref, kc_ref, kn_ref, vp_ref, vc_ref, vn_ref, kx_ref, vx_ref, sink_ref, o_ref):
    j = pl.program_id(2)
    g = C_HEADS // C_KV
    lane = lax.broadcasted_iota(jnp.int32, (TQ, 128), 1)
    lo = lane < C_DH
    zero = jnp.zeros((TQ, 128), BF16)
    k_all = jnp.concatenate([kx_ref[...], kp_ref[...], kc_ref[...], kn_ref[...]], axis=0)
    v_ext = _ones_ext(jnp.concatenate([vx_ref[...], vp_ref[...], vc_ref[...], vn_ref[...]], axis=0))
    r = lax.broadcasted_iota(jnp.int32, (TQ, TQ), 0)
    far = 4 * TQ
    is_lat = j < NQ_LAT
    off_p = jnp.where(jnp.logical_and(is_lat, j > 0), 0, far)
    off_c = jnp.where(is_lat, 0, far)
    off_n = jnp.where(j < NQ_LAT - 1, 0, far)
    bias = jnp.concatenate([
        jnp.zeros((TQ, CTX), F32),
        jnp.where(lane >= r + off_p, 0.0, NEG_INF),
        jnp.where(lane >= off_c, 0.0, NEG_INF),
        jnp.where(lane <= r - off_n, 0.0, NEG_INF)], axis=1)
    heads = []
    for p in range(g // 2):
        qb = q_ref[:, p * 128:(p + 1) * 128]
        heads += [jnp.where(lo, qb, zero), jnp.where(lo, zero, qb)]
    scores = [_qk(qh, k_all) + bias for qh in heads]
    outs = []
    for h in range(g):
        s = scores[h]
        sink = jnp.tile(sink_ref[h * 8:(h + 1) * 8, :], (TQ // 8, 1))
        m_part = sink
        for c in range(s.shape[1] // 128):
            m_part = jnp.maximum(m_part, s[:, c * 128:(c + 1) * 128])
        m = jnp.broadcast_to(jnp.max(m_part, axis=-1, keepdims=True), (TQ, 128))
        p = jnp.exp2(s - jnp.tile(m, (1, s.shape[1] // 128))).astype(BF16)
        pv = jnp.dot(p, v_ext, preferred_element_type=F32)
        outs.append(pv[:, 0:128] / (pv[:, 128:] + jnp.exp2(sink - m)))
    for p in range(g // 2):
        o_ref[:, p * 128:(p + 1) * 128] = jnp.where(lo, outs[2 * p], outs[2 * p + 1]).astype(BF16)


def _attn_c(qkv, sink_rows):
    nq = NQ_LAT + NQ_CTX
    qcols = C_HEADS * C_DH // C_KV
    kcol = C_HEADS * C_DH // 128
    vcol = kcol + C_KV
    n_lat_blocks = T_LAT // TQ

    def q_idx(b, kv, j):
        return (jnp.where(j < NQ_LAT, b * NQ_LAT + j, n_lat_blocks + b * NQ_CTX + (j - NQ_LAT)), kv)

    def loc_idx(delta, col):
        def idx(b, kv, j):
            blk = b * NQ_LAT + jnp.minimum(j, NQ_LAT - 1) + delta
            return (jnp.clip(blk, 0, n_lat_blocks - 1), col + kv)
        return idx

    loc = lambda delta, col: pl.BlockSpec((TQ, 128), loc_idx(delta, col))
    ctx = lambda col: pl.BlockSpec((CTX, 128), lambda b, kv, j: (T_LAT // CTX + b, col + kv))
    return pl.pallas_call(
        _attn_c_kernel,
        grid=(B, C_KV, nq),
        in_specs=[
            pl.BlockSpec((TQ, qcols), q_idx),
            loc(-1, kcol), loc(0, kcol), loc(1, kcol),
            loc(-1, vcol), loc(0, vcol), loc(1, vcol),
            ctx(kcol), ctx(vcol),
            pl.BlockSpec((None, (C_HEADS // C_KV) * 8, 128), lambda b, kv, j: (kv, 0, 0)),
        ],
        out_specs=pl.BlockSpec((TQ, qcols), q_idx),
        out_shape=jax.ShapeDtypeStruct((T, D), BF16),
        compiler_params=_cparams(("parallel", "parallel", "arbitrary"), 40),
        name="attn_c",
    )(qkv, qkv, qkv, qkv, qkv, qkv, qkv, qkv, qkv, sink_rows)


def _proj_b_kernel(h_ref, hp_ref, hn_ref, mod_ref, g_ref, w_ref, b_ref, cw_ref, cb_ref,
                   x0_ref, z_ref, xn_ref, u_ref):
    i = pl.program_id(0)
    m = mod_ref[...]
    _fill_xn(xn_ref, h_ref, hp_ref, hn_ref, g_ref[0:1, :], m[:, 0:D], m[:, D:2 * D])
    has_prev, has_next = _seg_masks(i)

    def conv_group(k):
        cols = slice(k * D, (k + 1) * D)
        u_ref[...] = jnp.dot(xn_ref[...], w_ref[:, cols], preferred_element_type=F32) + b_ref[:, cols]
        return _conv3(u_ref, cw_ref[:, cols], cb_ref[:, cols], has_prev, has_next, TM)

    x0_ref[...] = conv_group(0).astype(BF16)
    x1 = conv_group(1)
    v = conv_group(2)
    z_ref[...] = (v * x1).astype(BF16)


def _proj_b(h, mods_l, g, w, b, cw, cb):
    full = lambda shape: pl.BlockSpec(shape, lambda i: (0,) * len(shape))
    return pl.pallas_call(
        _proj_b_kernel,
        grid=(N_TILES,),
        in_specs=_row_specs(T) + [
            pl.BlockSpec((None, 1, N_MOD * D), lambda i: (_mod_row(i), 0, 0)),
            full((4, D)), full((D, 3 * D)), full((1, 3 * D)), full((3, 3 * D)), full((1, 3 * D)),
        ],
        out_specs=[pl.BlockSpec((TM, D), lambda i: (i, 0))] * 2,
        out_shape=[jax.ShapeDtypeStruct((T, D), BF16)] * 2,
        scratch_shapes=[pltpu.VMEM((TM + 2 * HALO, D), BF16), pltpu.VMEM((TM + 2 * HALO, D), F32)],
        compiler_params=_cparams(("parallel",), 48),
        name="proj_b",
    )(h, h, h, mods_l, g, w, b, cw, cb)


def _filter_kernel(zf_ref, t_ref, w1_ref, b1_ref, w2_ref, b2_ref, w3_ref, b3_ref, w4_ref, rates_ref, o_ref):
    i = pl.program_id(0)
    tm = zf_ref.shape[0]
    dot = functools.partial(jnp.dot, preferred_element_type=F32, precision=lax.Precision.HIGHEST)
    f = jnp.sin(FILTER_SIN_FREQ * (dot(zf_ref[...], w1_ref[...]) + b1_ref[...]))
    f = jnp.sin(FILTER_SIN_FREQ * (dot(f, w2_ref[...]) + b2_ref[...]))
    f = jnp.sin(FILTER_SIN_FREQ * (dot(f, w3_ref[...]) + b3_ref[...]))
    f = dot(f, w4_ref[...])
    decay = jnp.exp(-t_ref[...] * rates_ref[...])
    row = i * tm + lax.broadcasted_iota(jnp.int32, (tm, 1), 0)
    o_ref[:, 0:D] = (f[:, 0:D] * decay).astype(BF16)
    o_ref[:, D:2 * D] = (f[:, D:2 * D] * decay * (row != 0).astype(F32)).astype(BF16)


def _hyena_filter(seq_len, w1, b1, w2, b2, w3, b3, w4):
    zf, t, rates = _filter_features(seq_len)
    tm = min(seq_len, 512)
    w1p = jnp.pad(w1, ((0, FILTER_WIDTH - FILTER_EMB), (0, 0)))
    full = lambda shape: pl.BlockSpec(shape, lambda i: (0,) * len(shape))
    fw = FILTER_WIDTH
    return pl.pallas_call(
        _filter_kernel,
        grid=(seq_len // tm,),
        in_specs=[
            pl.BlockSpec((tm, fw), lambda i: (i, 0)),
            pl.BlockSpec((tm, 1), lambda i: (i, 0)),
            full((fw, fw)), full((1, fw)), full((fw, fw)), full((1, fw)), full((fw, fw)), full((1, fw)),
            full((fw, 2 * D)), full((1, D)),
        ],
        out_specs=pl.BlockSpec((tm, 2 * D), lambda i: (i, 0)),
        out_shape=jax.ShapeDtypeStruct((seq_len, 2 * D), BF16),
        compiler_params=_cparams(("parallel",), 40),
        name="hyena_filter",
    )(zf, t, w1p, b1.reshape(1, fw), w2, b2.reshape(1, fw), w3, b3.reshape(1, fw), w4, rates)


def _dft_forward(x_ref, xs_ref, a_ref, m1_ref, tf_ref, emit):
    ct = x_ref.shape[1]
    xs_ref[...] = x_ref[...].astype(F32).reshape(FFT_H1, FFT_N2, ct)

    def stage1(g, carry):
        rows = pl.ds(pl.multiple_of(g * FFT_SUB, FFT_SUB), FFT_SUB)
        x = xs_ref[:, rows, :].reshape(FFT_H1 * FFT_SUB, ct).astype(BF16)
        a = jnp.dot(m1_ref[...], x, preferred_element_type=F32)
        a_ref[:, rows, :] = a.reshape(FFT_R, FFT_SUB, ct)
        return carry

    lax.fori_loop(0, FFT_N2 // FFT_SUB, stage1, 0)

    def stage2(k1, carry):
        d = a_ref[pl.ds(2 * k1, 2)].reshape(2 * FFT_N2, ct).astype(BF16)
        emit(k1, jnp.dot(tf_ref[k1], d, preferred_element_type=F32))
        return carry

    lax.fori_loop(0, FFT_K1, stage2, 0, unroll=FFT_UNROLL)


def _filter_spectrum_kernel(hf_ref, hb_ref, m1_ref, tf_ref, o_ref, xs_ref, a_ref, f_ref):
    half = FFT_N2

    def keep(k1, x):
        f_ref[k1] = x

    _dft_forward(hf_ref, xs_ref, a_ref, m1_ref, tf_ref, keep)

    def combine(k1, x):
        f = f_ref[k1]
        o_ref[k1, 0:half, :] = (f[0:half] + x[0:half]).astype(BF16)
        o_ref[k1, half:, :] = (f[half:] - x[half:]).astype(BF16)

    _dft_forward(hb_ref, xs_ref, a_ref, m1_ref, tf_ref, combine)


def _long_conv_kernel(z_ref, hf_ref, m1_ref, tf_ref, ti_ref, m1i_ref, y_ref, xs_ref, a_ref, g_ref):
    half = FFT_N2
    ct = z_ref.shape[1]
    g_ref[2 * FFT_K1:] = jnp.zeros((FFT_R - 2 * FFT_K1, half, ct), F32)

    def product_and_invert(k1, x):
        hr = hf_ref[k1, 0:half, :].astype(F32)
        hi = hf_ref[k1, half:, :].astype(F32)
        xr, xi = x[0:half], x[half:]
        p = jnp.concatenate([xr * hr - xi * hi, xr * hi + xi * hr], axis=0).astype(BF16)
        g_ref[pl.ds(2 * k1, 2)] = jnp.dot(ti_ref[k1], p, preferred_element_type=F32).reshape(2, half, ct)

    _dft_forward(z_ref, xs_ref, a_ref, m1_ref, tf_ref, product_and_invert)

    def inverse_stage1(g, carry):
        rows = pl.ds(pl.multiple_of(g * FFT_SUB, FFT_SUB), FFT_SUB)
        x = g_ref[:, rows, :].reshape(FFT_R * FFT_SUB, ct).astype(BF16)
        y = jnp.dot(m1i_ref[...], x, preferred_element_type=F32)
        xs_ref[:, rows, :] = y.reshape(FFT_H1, FFT_SUB, ct)
        return carry

    lax.fori_loop(0, FFT_N2 // FFT_SUB, inverse_stage1, 0)
    y_ref[...] = xs_ref[...].reshape(L, ct).astype(y_ref.dtype)


def _long_conv_latent(z, filt):
    m1, m1i, tf, ti = _two_stage_dft()
    ct = FFT_CT
    n_ct = D // ct
    const = lambda shape: pl.BlockSpec(shape, lambda *_: (0,) * len(shape))
    scratch = [pltpu.VMEM((FFT_H1, FFT_N2, ct), F32), pltpu.VMEM((FFT_R, FFT_N2, ct), F32)]
    hf = pl.pallas_call(
        _filter_spectrum_kernel,
        grid=(n_ct,),
        in_specs=[
            pl.BlockSpec((L, ct), lambda c: (0, c)),
            pl.BlockSpec((L, ct), lambda c: (0, n_ct + c)),
            const(m1.shape), const(tf.shape),
        ],
        out_specs=pl.BlockSpec((FFT_K1, 2 * FFT_N2, ct), lambda c: (0, 0, c)),
        out_shape=jax.ShapeDtypeStruct((FFT_K1, 2 * FFT_N2, D), BF16),
        scratch_shapes=scratch + [pltpu.VMEM((FFT_K1, 2 * FFT_N2, ct), F32)],
        compiler_params=_cparams(("parallel",), 48),
        name="filter_spectrum",
    )(filt, filt, m1, tf)
    return pl.pallas_call(
        _long_conv_kernel,
        grid=(n_ct, B),
        in_specs=[
            pl.BlockSpec((L, ct), lambda c, b: (b, c)),
            pl.BlockSpec((FFT_K1, 2 * FFT_N2, ct), lambda c, b: (0, 0, c)),
            const(m1.shape), const(tf.shape), const(ti.shape), const(m1i.shape),
        ],
        out_specs=pl.BlockSpec((L, ct), lambda c, b: (b, c)),
        out_shape=jax.ShapeDtypeStruct((T_LAT, D), BF16),
        scratch_shapes=scratch + [pltpu.VMEM((FFT_R, FFT_N2, ct), F32)],
        compiler_params=_cparams(("parallel", "arbitrary"), 56),
        name="long_conv",
    )(z, hf, m1, tf, ti, m1i)


def _group_mm_kernel(t_ref, d_ref, *rest, mode, gt):
    if mode == "product":
        hf_ref, o_ref = rest
    else:
        (o_ref,) = rest
    half = t_ref.shape[1] // 2

    def body(g, carry):
        x = jnp.dot(t_ref[g], d_ref[g], preferred_element_type=F32)
        if mode == "none":
            o_ref[g] = x.astype(o_ref.dtype)
        elif mode == "combine":
            o_ref[g, 0:half, :] = (x[0:half, 0:D] + x[0:half, D:2 * D]).astype(o_ref.dtype)
            o_ref[g, half:, :] = (x[half:, 0:D] - x[half:, D:2 * D]).astype(o_ref.dtype)
        else:
            hr = hf_ref[g, 0:half, :].astype(F32)
            hi = hf_ref[g, half:, :].astype(F32)
            xr, xi = x[0:half], x[half:]
            o_ref[g, 0:half, :] = (xr * hr - xi * hi).astype(o_ref.dtype)
            o_ref[g, half:, :] = (xr * hi + xi * hr).astype(o_ref.dtype)
        return carry

    lax.fori_loop(0, gt, body, 0)


def _group_mm(tmat, data, n_batch, b_off, mode, out_dtype, hf=None):
    n_g, r, k = tmat.shape
    c = data.shape[3]
    gt = n_g
    c_out = D if mode == "combine" else c
    in_specs = [
        pl.BlockSpec((gt, r, k), lambda gi, b: (gi, 0, 0)),
        pl.BlockSpec((None, gt, k, c), lambda gi, b: (b + b_off, gi, 0, 0)),
    ]
    args = [tmat, data]
    if mode == "product":
        in_specs.append(pl.BlockSpec((gt, r, c), lambda gi, b: (gi, 0, 0)))
        args.append(hf)
    return pl.pallas_call(
        functools.partial(_group_mm_kernel, mode=mode, gt=gt),
        grid=(n_g // gt, n_batch),
        in_specs=in_specs,
        out_specs=pl.BlockSpec((None, gt, r, c_out), lambda gi, b: (b, gi, 0, 0)),
        out_shape=jax.ShapeDtypeStruct((n_batch, n_g, r, c_out), out_dtype),
        compiler_params=_cparams(("parallel", "arbitrary"), 48),
        name="dft_stage2_" + mode,
    )(*args)


def _hyena_long_conv(z, filt_lat, filt_ctx):
    y_lat = _long_conv_latent(z, filt_lat)
    fc, fci = _direct_dft(CTX)
    hfc = _group_mm(fc, filt_ctx.reshape(1, 1, CTX, 2 * D), 1, 0, "combine", BF16)[0]
    pc = _group_mm(fc, z.reshape(T // CTX, 1, CTX, D), B, T_LAT // CTX, "product", BF16, hfc)
    y_ctx = _group_mm(fci, pc, B, 0, "none", BF16).reshape(T_CTX, D)
    return y_lat, y_ctx


def _outproj_kernel(h_ref, mod_ref, g_ref, w_ref, b_ref, *rest, hyena, split):
    i = pl.program_id(0)
    m = mod_ref[...]
    for r0 in range(0, TM, SUB_ROWS):
        rows = slice(r0, r0 + SUB_ROWS)
        if split:
            y = jnp.where(i < N_LAT_TILES, rest[0][rows, :], rest[1][rows, :])
            tail = rest[2:]
        else:
            y = rest[0][rows, :]
            tail = rest[1:]
        if hyena:
            z_ref, x0_ref, skip_ref, o_ref = tail
            y = ((y.astype(F32) + z_ref[rows, :].astype(F32) * skip_ref[...])
                 * x0_ref[rows, :].astype(F32)).astype(BF16)
        else:
            (o_ref,) = tail
        r = jnp.dot(y, w_ref[...], preferred_element_type=F32) + b_ref[...]
        o_ref[rows, :] = h_ref[rows, :] + m[:, 2 * D:3 * D] * _rms(r, g_ref[1:2, :])


def _outproj(h, mods_l, g, w, b, y_lat, y_ctx, n_tiles, hyena_args=None):
    tile = lambda i: (i, 0)
    in_specs = [
        pl.BlockSpec((TM, D), tile),
        pl.BlockSpec((None, 1, N_MOD * D), lambda i: (_mod_row(i), 0, 0)),
        pl.BlockSpec((4, D), lambda i: (0, 0)),
        pl.BlockSpec((D, D), lambda i: (0, 0)),
        pl.BlockSpec((1, D), lambda i: (0, 0)),
    ]
    split = y_ctx is not None
    if split:
        in_specs += [
            pl.BlockSpec((TM, D), lambda i: (jnp.minimum(i, N_LAT_TILES - 1), 0)),
            pl.BlockSpec((TM, D), lambda i: (jnp.maximum(i - N_LAT_TILES, 0), 0)),
        ]
        args = [y_lat, y_ctx]
    else:
        in_specs.append(pl.BlockSpec((TM, D), tile))
        args = [y_lat]
    hyena = hyena_args is not None
    if hyena:
        in_specs += [pl.BlockSpec((TM, D), tile), pl.BlockSpec((TM, D), tile), pl.BlockSpec((1, D), lambda i: (0, 0))]
        args += list(hyena_args)
    return pl.pallas_call(
        functools.partial(_outproj_kernel, hyena=hyena, split=split),
        grid=(n_tiles,),
        in_specs=in_specs,
        out_specs=pl.BlockSpec((TM, D), tile),
        out_shape=jax.ShapeDtypeStruct((n_tiles * TM, D), F32),
        compiler_params=_cparams(("parallel",), 40),
        name="outproj",
    )(h, mods_l, g, w, b, *args)


def _ffn_kernel(h_ref, hp_ref, hn_ref, mod_ref, g_ref, wup_ref, cw_ref, cb_ref, wd_ref, o_ref, xn_ref, gate_ref):
    i = pl.program_id(0)
    m = mod_ref[...]
    _fill_xn(xn_ref, h_ref, hp_ref, hn_ref, g_ref[2:3, :], m[:, 3 * D:4 * D], m[:, 4 * D:5 * D])
    has_prev, has_next = _seg_masks(i)
    ext = SUB_ROWS + 2 * HALO
    n_sb = TM // SUB_ROWS
    units = [(c, sb) for c in range(N_FC) for sb in range(n_sb)]

    def up_matmuls(k):
        c, sb = units[k]
        r0 = sb * SUB_ROWS
        gate_ref[k % 2] = jnp.dot(xn_ref[r0:r0 + ext, :], wup_ref[:, c * FC:(c + 1) * FC],
                                  preferred_element_type=F32)
        return jnp.dot(xn_ref[HALO + r0:HALO + r0 + SUB_ROWS, :],
                       wup_ref[:, D_FF + c * FC:D_FF + (c + 1) * FC], preferred_element_type=F32)

    acc = [None] * n_sb
    up_next = up_matmuls(0)
    for k, (c, sb) in enumerate(units):
        up = up_next
        if k + 1 < len(units):
            up_next = up_matmuls(k + 1)
        cols = slice(c * FC, (c + 1) * FC)
        rows = slice(sb * SUB_ROWS, (sb + 1) * SUB_ROWS)
        conv = _conv3(gate_ref.at[k % 2], cw_ref[:, cols], cb_ref[:, cols], has_prev[rows], has_next[rows], SUB_ROWS)
        part = jnp.dot((_silu(conv) * up).astype(BF16), wd_ref[cols, :], preferred_element_type=F32)
        acc[sb] = part if c == 0 else acc[sb] + part
    for sb in range(TM // SUB_ROWS):
        rows = slice(sb * SUB_ROWS, (sb + 1) * SUB_ROWS)
        o_ref[rows, :] = h_ref[rows, :] + m[:, 5 * D:6 * D] * _rms(acc[sb], g_ref[3:4, :])


def _ffn(h, mods_l, g, w_up, cw, cb, w_down, n_tiles):
    resident = lambda shape: pl.BlockSpec(shape, lambda i: (0,) * len(shape), pipeline_mode=pl.Buffered(1))
    return pl.pallas_call(
        _ffn_kernel,
        grid=(n_tiles,),
        in_specs=_row_specs(h.shape[0]) + [
            pl.BlockSpec((None, 1, N_MOD * D), lambda i: (_mod_row(i), 0, 0)),
            pl.BlockSpec((4, D), lambda i: (0, 0)),
            resident((D, 2 * D_FF)), resident((3, D_FF)), resident((1, D_FF)), resident((D_FF, D)),
        ],
        out_specs=pl.BlockSpec((TM, D), lambda i: (i, 0)),
        out_shape=jax.ShapeDtypeStruct((n_tiles * TM, D), F32),
        scratch_shapes=[
            pltpu.VMEM((TM + 2 * HALO, D), BF16),
            pltpu.VMEM((2, SUB_ROWS + 2 * HALO, FC), F32),
        ],
        compiler_params=_cparams(("parallel",), 56),
        name="ffn",
    )(h, h, h, mods_l, g, w_up, cw, cb, w_down)


def _dup_heads(w):
    k0, k1 = w[..., 0:C_DH], w[..., C_DH:2 * C_DH]
    return jnp.concatenate([k0, k0, k1, k1], axis=-1)


def kernel(x, c, ctx, c_ctx, mod_w, mod_b, norm_g, ffn_w_up, ffn_conv_w, ffn_conv_b, ffn_w_down, a_w_qkv, a_q_gain, a_k_gain, a_w_out, b_w_in, b_b_in, b_conv_w, b_conv_b, b_f_w1, b_f_b1, b_f_w2, b_f_b2, b_f_w3, b_f_b3, b_f_w4, b_skip, b_w_out, b_b_out, c_w_qkv, c_b_qkv, c_sink, c_w_out, c_b_out):
    h = jnp.concatenate([x.reshape(T_LAT, D), ctx.reshape(T_CTX, D)], axis=0)
    cs = jnp.concatenate([c, c_ctx[None], jnp.zeros((8 - B - 1, D), F32)], axis=0)
    mods = _mods(cs, mod_w, mod_b).reshape(DEPTH, 8, 1, N_MOD * D)
    cos_a, sin_a = _rope_tables(A_DH)
    cos_c, sin_c = _rope_tables(C_DH)
    zeros_d = jnp.zeros((1, D), F32)

    for i in range(DEPTH):
        last = i == DEPTH - 1
        kind, j = i % 3, i // 3
        g = norm_g[i]
        mods_l = mods[i]
        n_tiles = N_LAT_TILES if last else N_TILES
        if kind == 0:
            qkv = _proj_attn(h, mods_l, g, a_w_qkv[j].astype(BF16), jnp.zeros((1, QKV_COLS), F32),
                             a_q_gain[j].reshape(1, A_DH), a_k_gain[j].reshape(1, A_DH), cos_a, sin_a, "a")
            o_lat = _attn_a(qkv, latent=True)
            o_ctx = None if last else _attn_a(qkv, latent=False)
            h = _outproj(h, mods_l, g, a_w_out[j].astype(BF16), zeros_d, o_lat, o_ctx, n_tiles)
        elif kind == 1:
            x0, z = _proj_b(h, mods_l, g, b_w_in[j].astype(BF16), b_b_in[j].reshape(1, 3 * D),
                            b_conv_w[j], b_conv_b[j].reshape(1, 3 * D))
            fw = (b_f_w1[j], b_f_b1[j], b_f_w2[j], b_f_b2[j], b_f_w3[j], b_f_b3[j], b_f_w4[j])
            y_lat, y_ctx = _hyena_long_conv(z, _hyena_filter(L, *fw), _hyena_filter(CTX, *fw))
            h = _outproj(h, mods_l, g, b_w_out[j].astype(BF16), b_b_out[j].reshape(1, D), y_lat, y_ctx, n_tiles,
                         hyena_args=(z, x0, b_skip[j].reshape(1, D)))
        else:
            qc = C_HEADS * C_DH
            kc = qc + C_KV * C_DH
            w, bias = c_w_qkv[j], c_b_qkv[j].reshape(1, -1)
            w = jnp.concatenate([w[:, :qc], _dup_heads(w[:, qc:kc]), _dup_heads(w[:, kc:])], axis=1)
            bias = jnp.concatenate([bias[:, :qc], _dup_heads(bias[:, qc:kc]), _dup_heads(bias[:, kc:])], axis=1)
            ones = jnp.ones((1, 128), F32)
            qkv = _proj_attn(h, mods_l, g, w.astype(BF16), bias, ones, ones, cos_c, sin_c, "c")
            sink_rows = jnp.broadcast_to((c_sink[j].astype(F32) * LOG2_E)[:, None, None],
                                         (C_HEADS, 8, 128)).reshape(C_KV, (C_HEADS // C_KV) * 8, 128)
            o = _attn_c(qkv, sink_rows)
            h = _outproj(h, mods_l, g, c_w_out[j].astype(BF16), c_b_out[j].reshape(1, D), o, None, n_tiles)
        h = _ffn(h, mods_l, g, ffn_w_up[i].astype(BF16), ffn_conv_w[i], ffn_conv_b[i].reshape(1, D_FF),
                 ffn_w_down[i].astype(BF16), n_tiles)
    return h.reshape(B, L, D)
```

```python
import functools
import math

import numpy as np
import jax
import jax.numpy as jnp
from jax import lax
from jax.experimental import pallas as pl
from jax.experimental.pallas import tpu as pltpu

F32 = jnp.float32
BF16 = jnp.bfloat16

D = 1024
B = 4
L = 4096
CTX = 256
DEPTH = 4
N_MOD = 6
EPS = 1e-6
NEG_INF = -1e30
LOG2_E = math.log2(math.e)
GRID_W = 64
ROPE_THETA = 10000.0

A_HEADS, A_KV, A_DH = 8, 2, 128
C_HEADS, C_KV, C_DH = 16, 2, 64
QKV_COLS = 1536
PROJ_GROUP = 4
SUB_ROWS = 256
D_FF = 2816
FILTER_BANDS = 16
FILTER_EMB = 1 + 2 * FILTER_BANDS
FILTER_WIDTH = 64
FILTER_SIN_FREQ = 1.0
DECAY_TARGET = 1e-2
DECAY_FAST_PCT = 0.3
DECAY_SLOW_PCT = 1.5

T_LAT = B * L
T_CTX = B * CTX
T = T_LAT + T_CTX

TM = 512
HALO = 16
N_LAT_TILES = T_LAT // TM
N_TILES = T // TM
FC = 1408
N_FC = D_FF // FC
TQ = 128
TQA = 256
TK = 2048
NQ_LAT = L // TQ
NQ_CTX = CTX // TQ

FFT_N = 2 * L
FFT_N1 = 128
FFT_N2 = FFT_N // FFT_N1
FFT_H1 = FFT_N1 // 2
FFT_K1 = FFT_H1 + 1
FFT_R = 144
FFT_CT = 256
FFT_SUB = 8
FFT_UNROLL = 13

VMEM_MB = 1024 * 1024


def _cparams(sem, vmem_mb):
    return pltpu.CompilerParams(dimension_semantics=sem, vmem_limit_bytes=vmem_mb * VMEM_MB)


def _rope_tables(head_dim):
    rows = L // GRID_W
    row = np.repeat(np.arange(rows), GRID_W).astype(np.float64)
    col = np.tile(np.arange(GRID_W), rows).astype(np.float64)
    axis_dim = head_dim // 2
    inv = np.power(ROPE_THETA, -np.arange(0, axis_dim, 2, dtype=np.float64) / axis_dim)
    ang = np.concatenate([row[:, None] * inv[None], col[:, None] * inv[None]], axis=-1)
    cos, sin = np.cos(ang), np.sin(ang)
    reps = 128 // head_dim
    cos_t = np.tile(np.concatenate([cos, cos], axis=-1), (1, reps))
    sin_t = np.tile(np.concatenate([-sin, sin], axis=-1), (1, reps))
    cos_t = np.concatenate([cos_t, np.ones((TM, 128))], axis=0)
    sin_t = np.concatenate([sin_t, np.zeros((TM, 128))], axis=0)
    return jnp.asarray(cos_t, F32), jnp.asarray(sin_t, F32)


def _two_stage_dft():
    n1s, n2s, h1, k1n, n = FFT_N1, FFT_N2, FFT_H1, FFT_K1, FFT_N
    n1 = np.arange(h1)
    k1 = np.arange(k1n)
    ang = 2 * np.pi * np.outer(k1, n1) / n1s
    m1 = np.zeros((FFT_R, h1))
    m1[0:2 * k1n:2] = np.cos(ang)
    m1[1:2 * k1n:2] = -np.sin(ang)
    c = np.where((k1 == 0) | (k1 == h1), 1.0, 2.0)
    m1i = np.zeros((h1, FFT_R))
    m1i[:, 0:2 * k1n:2] = (c[None] / n) * np.cos(ang.T)
    m1i[:, 1:2 * k1n:2] = -(c[None] / n) * np.sin(ang.T)
    n2 = np.arange(n2s)
    k2 = np.arange(n2s)
    tf = np.zeros((k1n, 2 * n2s, 2 * n2s))
    ti = np.zeros((k1n, 2 * n2s, 2 * n2s))
    for a in range(k1n):
        th = 2 * np.pi * np.outer(a + n1s * k2, n2) / n
        er, ei = np.cos(th), -np.sin(th)
        tf[a] = np.block([[er, -ei], [ei, er]])
        er, ei = np.cos(th.T), np.sin(th.T)
        ti[a] = np.block([[er, -ei], [ei, er]])
    eye = np.eye(FFT_SUB)
    return tuple(jnp.asarray(a, F32).astype(BF16) for a in (np.kron(m1, eye), np.kron(m1i, eye), tf, ti))


def _direct_dft(seq_len):
    n = 2 * seq_len
    th = 2 * np.pi * np.outer(np.arange(n), np.arange(seq_len)) / n
    fwd = np.concatenate([np.cos(th), -np.sin(th)], axis=0)
    inv = np.concatenate([np.cos(th.T), -np.sin(th.T)], axis=1) / n
    return jnp.asarray(fwd, F32).astype(BF16), jnp.asarray(inv, F32).astype(BF16)


def _filter_features(seq_len):
    t = np.linspace(0.0, 1.0, seq_len)[:, None]
    omega = (2.0 * math.pi / seq_len) * np.arange(seq_len, dtype=np.float64)
    bands = np.linspace(1e-4, FILTER_BANDS - 1, FILTER_BANDS)
    ang = omega[:, None] * bands[None, :]
    z = np.concatenate([t, np.cos(ang), -np.sin(ang)], axis=-1)
    z = np.pad(z, ((0, 0), (0, FILTER_WIDTH - FILTER_EMB)))
    rates = np.abs(np.linspace(math.log(DECAY_TARGET) / DECAY_FAST_PCT,
                               math.log(DECAY_TARGET) / DECAY_SLOW_PCT, D))[None]
    return jnp.asarray(z, F32), jnp.asarray(t, F32), jnp.asarray(rates, F32)


def _rms(x, g):
    return x * lax.rsqrt(jnp.mean(x * x, axis=-1, keepdims=True) + EPS) * g


def _norm_mod(h, g, shift, scale):
    return _rms(h, g) * (1.0 + scale) + shift


def _silu(x):
    return x * (1.0 / (1.0 + jnp.exp(-x)))


def _seg_masks(i):
    rows = i * TM + lax.broadcasted_iota(jnp.int32, (TM, 1), 0)
    seg = jnp.where(i < N_LAT_TILES, L, CTX)
    pos = rows & (seg - 1)
    return (pos != 0).astype(F32), (pos != seg - 1).astype(F32)


def _fill_xn(xn_ref, h_ref, hp_ref, hn_ref, g, shift, scale):
    xn_ref[0:HALO, :] = _norm_mod(hp_ref[...], g, shift, scale).astype(BF16)
    xn_ref[HALO:HALO + TM, :] = _norm_mod(h_ref[...], g, shift, scale).astype(BF16)
    xn_ref[HALO + TM:, :] = _norm_mod(hn_ref[...], g, shift, scale).astype(BF16)


def _conv3(u_ref, cw, cb, has_prev, has_next, n_rows):
    u = u_ref[...]
    total = n_rows + 2 * HALO
    u_prev = pltpu.roll(u, 1, 0)
    u_next = pltpu.roll(u, total - 1, 0)

    def piece(r0, r1, masked):
        prev = u_prev[HALO + r0:HALO + r1]
        nxt = u_next[HALO + r0:HALO + r1]
        if masked:
            prev = prev * has_prev[r0:r1]
            nxt = nxt * has_next[r0:r1]
        return cb + cw[0:1] * prev + cw[1:2] * u[HALO + r0:HALO + r1] + cw[2:3] * nxt

    pieces = []
    for s in range(0, n_rows, CTX):
        pieces += [piece(s, s + 8, True), piece(s + 8, s + CTX - 8, False), piece(s + CTX - 8, s + CTX, True)]
    return jnp.concatenate(pieces, axis=0)


def _mod_row(i):
    return jnp.where(i < N_LAT_TILES, i // (L // TM), B)


def _row_specs(in_rows):
    last = in_rows // HALO - 1
    per = TM // HALO
    return [
        pl.BlockSpec((TM, D), lambda i, *_: (i, 0)),
        pl.BlockSpec((HALO, D), lambda i, *_: (jnp.maximum(i * per - 1, 0), 0)),
        pl.BlockSpec((HALO, D), lambda i, *_: (jnp.minimum((i + 1) * per, last), 0)),
    ]


def _mods_kernel(s_ref, w_ref, b_ref, o_ref):
    s = _silu(s_ref[...])
    o_ref[...] = jnp.dot(s, w_ref[...], preferred_element_type=F32,
                         precision=lax.Precision.HIGHEST) + b_ref[...]


def _mods(cs, mod_w, mod_b):
    tn = 1536
    return pl.pallas_call(
        _mods_kernel,
        grid=(DEPTH, N_MOD * D // tn),
        in_specs=[
            pl.BlockSpec((8, D), lambda l, n: (0, 0)),
            pl.BlockSpec((None, D, tn), lambda l, n: (l, 0, n)),
            pl.BlockSpec((None, 1, tn), lambda l, n: (l, 0, n)),
        ],
        out_specs=pl.BlockSpec((None, 8, tn), lambda l, n: (l, 0, n)),
        out_shape=jax.ShapeDtypeStruct((DEPTH, 8, N_MOD * D), F32),
        compiler_params=_cparams(("parallel", "parallel"), 40),
        name="mods",
    )(cs, mod_w, mod_b.reshape(DEPTH, 1, N_MOD * D))


def _proj_attn_kernel(h_ref, mod_ref, g_ref, w_ref, b_ref, qg_ref, kg_ref, cos_ref, sin_ref, o_ref, *, kind):
    m = mod_ref[...]
    n_q = 8
    n_rope = 10
    q_scale = (A_DH if kind == "a" else C_DH) ** -0.5 * LOG2_E
    if kind == "c":
        lane = lax.broadcasted_iota(jnp.int32, (SUB_ROWS, 128), 1)
        first_half = (lane & (C_DH - 1)) < C_DH // 2
    for r0 in range(0, TM, SUB_ROWS):
        rows = slice(r0, r0 + SUB_ROWS)
        xn = _norm_mod(h_ref[rows, :], g_ref[0:1, :], m[:, 0:D], m[:, D:2 * D]).astype(BF16)
        cos, sin = cos_ref[rows, :], sin_ref[rows, :]
        cos_q, sin_q = cos * q_scale, sin * q_scale
        for j in range(QKV_COLS // 128):
            if j % PROJ_GROUP == 0:
                wide = slice(j * 128, (j + PROJ_GROUP) * 128)
                y_wide = jnp.dot(xn, w_ref[:, wide], preferred_element_type=F32) + b_ref[:, wide]
            y = y_wide[:, (j % PROJ_GROUP) * 128:(j % PROJ_GROUP + 1) * 128]
            if j < n_rope:
                if kind == "a":
                    y = _rms(y, qg_ref[...] if j < n_q else kg_ref[...])
                    rot = pltpu.roll(y, 64, 1)
                else:
                    rot = jnp.where(first_half, pltpu.roll(y, 128 - C_DH // 2, 1), pltpu.roll(y, C_DH // 2, 1))
                y = y * cos_q + rot * sin_q if j < n_q else y * cos + rot * sin
            o_ref[rows, j * 128:(j + 1) * 128] = y.astype(BF16)


def _proj_attn(h, mods_l, g, w, b, q_gain, k_gain, cos_t, sin_t, kind):
    tpb = L // TM
    rope_idx = lambda i: (jnp.where(i < N_LAT_TILES, i % tpb, tpb), 0)
    return pl.pallas_call(
        functools.partial(_proj_attn_kernel, kind=kind),
        grid=(N_TILES,),
        in_specs=[
            pl.BlockSpec((TM, D), lambda i: (i, 0)),
            pl.BlockSpec((None, 1, N_MOD * D), lambda i: (_mod_row(i), 0, 0)),
            pl.BlockSpec((4, D), lambda i: (0, 0)),
            pl.BlockSpec((D, QKV_COLS), lambda i: (0, 0)),
            pl.BlockSpec((1, QKV_COLS), lambda i: (0, 0)),
            pl.BlockSpec((1, 128), lambda i: (0, 0)),
            pl.BlockSpec((1, 128), lambda i: (0, 0)),
            pl.BlockSpec((TM, 128), rope_idx),
            pl.BlockSpec((TM, 128), rope_idx),
        ],
        out_specs=pl.BlockSpec((TM, QKV_COLS), lambda i: (i, 0)),
        out_shape=jax.ShapeDtypeStruct((T, QKV_COLS), BF16),
        compiler_params=_cparams(("parallel",), 40),
        name="proj_" + kind,
    )(h, mods_l, g, w, b, q_gain, k_gain, cos_t, sin_t)


def _qk(q, k):
    return lax.dot_general(q, k, (((1,), (1,)), ((), ())), preferred_element_type=F32)


def _ones_ext(v):
    return jnp.concatenate([v, jnp.ones(v.shape, BF16)], axis=1)


def _attn_a_kernel(q_ref, *refs, latent):
    if latent:
        kl_ref, vl_ref, kc_ref, vc_ref, o_ref = refs
        chunks = [[(kc_ref[...], vc_ref[...]), (kl_ref[0:TK, :], vl_ref[0:TK, :])]]
        chunks += [[(kl_ref[c * TK:(c + 1) * TK, :], vl_ref[c * TK:(c + 1) * TK, :])] for c in range(1, L // TK)]
    else:
        kc_ref, vc_ref, _, o_ref = refs
        chunks = [[(kc_ref[...], vc_ref[...])]]
    g = A_HEADS // A_KV
    m = [None] * g
    acc = [None] * g
    for ci, pieces in enumerate(chunks):
        v_exts = [_ones_ext(v) for _, v in pieces]
        scores = [jnp.concatenate([_qk(q_ref[:, i * A_DH:(i + 1) * A_DH], k) for k, _ in pieces], axis=1)
                  for i in range(g)]
        for i in range(g):
            s = scores[i]
            m_cur = jnp.broadcast_to(jnp.max(s, axis=-1, keepdims=True), (TQA, 128))
            m_new = m_cur if ci == 0 else jnp.maximum(m[i], m_cur)
            p = jnp.exp2(s - jnp.tile(m_new, (1, s.shape[1] // 128))).astype(BF16)
            pv, off = None, 0
            for (k, _), v_ext in zip(pieces, v_exts):
                part = jnp.dot(p[:, off:off + k.shape[0]], v_ext, preferred_element_type=F32)
                pv = part if pv is None else pv + part
                off += k.shape[0]
            if ci == 0:
                acc[i] = pv
            else:
                acc[i] = jnp.tile(jnp.exp2(m[i] - m_new), (1, 2)) * acc[i] + pv
            m[i] = m_new
    for i in range(g):
        o_ref[:, i * A_DH:(i + 1) * A_DH] = (acc[i][:, 0:A_DH] / acc[i][:, A_DH:]).astype(BF16)


def _attn_a(qkv, out_rows, into=None):
    latent = into is None
    nq = (L if latent else CTX) // TQA
    q0 = 0 if latent else T_LAT // TQA
    gcols = (A_HEADS // A_KV) * A_DH // 128
    q_spec = pl.BlockSpec((TQA, gcols * 128), lambda b, kv, j: (q0 + b * nq + j, kv))
    in_specs = [q_spec]
    if latent:
        in_specs += [
            pl.BlockSpec((L, A_DH), lambda b, kv, j: (b, A_HEADS + kv)),
            pl.BlockSpec((L, A_DH), lambda b, kv, j: (b, A_HEADS + A_KV + kv)),
        ]
    in_specs += [
        pl.BlockSpec((CTX, A_DH), lambda b, kv, j: (T_LAT // CTX + b, A_HEADS + kv)),
        pl.BlockSpec((CTX, A_DH), lambda b, kv, j: (T_LAT // CTX + b, A_HEADS + A_KV + kv)),
    ]
    args = [qkv] * len(in_specs)
    aliases = {}
    if not latent:
        aliases = {len(in_specs): 0}
        in_specs.append(pl.BlockSpec(memory_space=pl.ANY))
        args.append(into)
    return pl.pallas_call(
        functools.partial(_attn_a_kernel, latent=latent),
        grid=(B, A_KV, nq),
        in_specs=in_specs,
        out_specs=q_spec,
        out_shape=jax.ShapeDtypeStruct((out_rows, D), BF16),
        input_output_aliases=aliases,
        compiler_params=_cparams(("parallel", "parallel", "arbitrary"), 48),
        name="attn_a_lat" if latent else "attn_a_ctx",
    )(*args)


def _attn_c_kernel(q_ref, kp_ref, kc_ref, kn_ref, vp_ref, vc_ref, vn_ref, kx_ref, vx_ref, sink_ref, o_ref):
    j = pl.program_id(2)
    g = C_HEADS // C_KV
    lane = lax.broadcasted_iota(jnp.int32, (TQ, 128), 1)
    lo = lane < C_DH
    zero = jnp.zeros((TQ, 128), BF16)
    k_all = jnp.concatenate([kx_ref[...], kp_ref[...], kc_ref[...], kn_ref[...]], axis=0)
    v_ext = _ones_ext(jnp.concatenate([vx_ref[...], vp_ref[...], vc_ref[...], vn_ref[...]], axis=0))
    r = lax.broadcasted_iota(jnp.int32, (TQ, TQ), 0)
    far = 4 * TQ
    is_lat = j < NQ_LAT
    off_p = jnp.where(jnp.logical_and(is_lat, j > 0), 0, far)
    off_c = jnp.where(is_lat, 0, far)
    off_n = jnp.where(j < NQ_LAT - 1, 0, far)
    bias = jnp.concatenate([
        jnp.zeros((TQ, CTX), F32),
        jnp.where(lane >= r + off_p, 0.0, NEG_INF),
        jnp.where(lane >= off_c, 0.0, NEG_INF),
        jnp.where(lane <= r - off_n, 0.0, NEG_INF)], axis=1)
    heads = []
    for p in range(g // 2):
        qb = q_ref[:, p * 128:(p + 1) * 128]
        heads += [jnp.where(lo, qb, zero), jnp.where(lo, zero, qb)]
    scores = [_qk(qh, k_all) + bias for qh in heads]
    outs = []
    for h in range(g):
        s = scores[h]
        sink = jnp.tile(sink_ref[h * 8:(h + 1) * 8, :], (TQ // 8, 1))
        m_part = sink
        for c in range(s.shape[1] // 128):
            m_part = jnp.maximum(m_part, s[:, c * 128:(c + 1) * 128])
        m = jnp.broadcast_to(jnp.max(m_part, axis=-1, keepdims=True), (TQ, 128))
        p = jnp.exp2(s - jnp.tile(m, (1, s.shape[1] // 128))).astype(BF16)
        pv = jnp.dot(p, v_ext, preferred_element_type=F32)
        outs.append(pv[:, 0:128] / (pv[:, 128:] + jnp.exp2(sink - m)))
    for p in range(g // 2):
        o_ref[:, p * 128:(p + 1) * 128] = jnp.where(lo, outs[2 * p], outs[2 * p + 1]).astype(BF16)


def _attn_c(qkv, sink_rows):
    nq = NQ_LAT + NQ_CTX
    qcols = C_HEADS * C_DH // C_KV
    kcol = C_HEADS * C_DH // 128
    vcol = kcol + C_KV
    n_lat_blocks = T_LAT // TQ

    def q_idx(b, kv, j):
        return (jnp.where(j < NQ_LAT, b * NQ_LAT + j, n_lat_blocks + b * NQ_CTX + (j - NQ_LAT)), kv)

    def loc_idx(delta, col):
        def idx(b, kv, j):
            blk = b * NQ_LAT + jnp.minimum(j, NQ_LAT - 1) + delta
            return (jnp.clip(blk, 0, n_lat_blocks - 1), col + kv)
        return idx

    loc = lambda delta, col: pl.BlockSpec((TQ, 128), loc_idx(delta, col))
    ctx = lambda col: pl.BlockSpec((CTX, 128), lambda b, kv, j: (T_LAT // CTX + b, col + kv))
    return pl.pallas_call(
        _attn_c_kernel,
        grid=(B, C_KV, nq),
        in_specs=[
            pl.BlockSpec((TQ, qcols), q_idx),
            loc(-1, kcol), loc(0, kcol), loc(1, kcol),
            loc(-1, vcol), loc(0, vcol), loc(1, vcol),
            ctx(kcol), ctx(vcol),
            pl.BlockSpec((None, (C_HEADS // C_KV) * 8, 128), lambda b, kv, j: (kv, 0, 0)),
        ],
        out_specs=pl.BlockSpec((TQ, qcols), q_idx),
        out_shape=jax.ShapeDtypeStruct((T, D), BF16),
        compiler_params=_cparams(("parallel", "parallel", "arbitrary"), 40),
        name="attn_c",
    )(qkv, qkv, qkv, qkv, qkv, qkv, qkv, qkv, qkv, sink_rows)


def _proj_b_kernel(h_ref, hp_ref, hn_ref, mod_ref, g_ref, w_ref, b_ref, cw_ref, cb_ref,
                   x0_ref, z_ref, xn_ref, u_ref):
    i = pl.program_id(0)
    m = mod_ref[...]
    _fill_xn(xn_ref, h_ref, hp_ref, hn_ref, g_ref[0:1, :], m[:, 0:D], m[:, D:2 * D])
    has_prev, has_next = _seg_masks(i)

    def matmul(k):
        u_ref[k % 2] = (jnp.dot(xn_ref[...], w_ref[:, k * D:(k + 1) * D], preferred_element_type=F32)
                        + b_ref[:, k * D:(k + 1) * D])

    matmul(0)
    x1 = None
    for k in range(3):
        if k + 1 < 3:
            matmul(k + 1)
        cols = slice(k * D, (k + 1) * D)
        conv = _conv3(u_ref.at[k % 2], cw_ref[:, cols], cb_ref[:, cols], has_prev, has_next, TM)
        if k == 0:
            x0_ref[...] = conv.astype(BF16)
        elif k == 1:
            x1 = conv
        else:
            z_ref[...] = (conv * x1).astype(BF16)


def _proj_b(h, mods_l, g, w, b, cw, cb):
    full = lambda shape: pl.BlockSpec(shape, lambda i: (0,) * len(shape))
    return pl.pallas_call(
        _proj_b_kernel,
        grid=(N_TILES,),
        in_specs=_row_specs(T) + [
            pl.BlockSpec((None, 1, N_MOD * D), lambda i: (_mod_row(i), 0, 0)),
            full((4, D)), full((D, 3 * D)), full((1, 3 * D)), full((3, 3 * D)), full((1, 3 * D)),
        ],
        out_specs=[pl.BlockSpec((TM, D), lambda i: (i, 0))] * 2,
        out_shape=[jax.ShapeDtypeStruct((T, D), BF16)] * 2,
        scratch_shapes=[pltpu.VMEM((TM + 2 * HALO, D), BF16), pltpu.VMEM((2, TM + 2 * HALO, D), F32)],
        compiler_params=_cparams(("parallel",), 48),
        name="proj_b",
    )(h, h, h, mods_l, g, w, b, cw, cb)


def _filter_kernel(zf_ref, t_ref, w1_ref, b1_ref, w2_ref, b2_ref, w3_ref, b3_ref, w4_ref, rates_ref, o_ref):
    i = pl.program_id(0)
    tm = zf_ref.shape[0]
    dot = functools.partial(jnp.dot, preferred_element_type=F32, precision=lax.Precision.HIGHEST)
    f = jnp.sin(FILTER_SIN_FREQ * (dot(zf_ref[...], w1_ref[...]) + b1_ref[...]))
    f = jnp.sin(FILTER_SIN_FREQ * (dot(f, w2_ref[...]) + b2_ref[...]))
    f = jnp.sin(FILTER_SIN_FREQ * (dot(f, w3_ref[...]) + b3_ref[...]))
    f = dot(f, w4_ref[...])
    decay = jnp.exp(-t_ref[...] * rates_ref[...])
    row = i * tm + lax.broadcasted_iota(jnp.int32, (tm, 1), 0)
    o_ref[:, 0:D] = (f[:, 0:D] * decay).astype(BF16)
    o_ref[:, D:2 * D] = (f[:, D:2 * D] * decay * (row != 0).astype(F32)).astype(BF16)


def _hyena_filter(seq_len, w1, b1, w2, b2, w3, b3, w4):
    zf, t, rates = _filter_features(seq_len)
    tm = min(seq_len, 512)
    w1p = jnp.pad(w1, ((0, FILTER_WIDTH - FILTER_EMB), (0, 0)))
    full = lambda shape: pl.BlockSpec(shape, lambda i: (0,) * len(shape))
    fw = FILTER_WIDTH
    return pl.pallas_call(
        _filter_kernel,
        grid=(seq_len // tm,),
        in_specs=[
            pl.BlockSpec((tm, fw), lambda i: (i, 0)),
            pl.BlockSpec((tm, 1), lambda i: (i, 0)),
            full((fw, fw)), full((1, fw)), full((fw, fw)), full((1, fw)), full((fw, fw)), full((1, fw)),
            full((fw, 2 * D)), full((1, D)),
        ],
        out_specs=pl.BlockSpec((tm, 2 * D), lambda i: (i, 0)),
        out_shape=jax.ShapeDtypeStruct((seq_len, 2 * D), BF16),
        compiler_params=_cparams(("parallel",), 40),
        name="hyena_filter",
    )(zf, t, w1p, b1.reshape(1, fw), w2, b2.reshape(1, fw), w3, b3.reshape(1, fw), w4, rates)


def _dft_forward(x_ref, xs_ref, a_ref, m1_ref, tf_ref, emit):
    ct = x_ref.shape[1]
    xs_ref[...] = x_ref[...].astype(F32).reshape(FFT_H1, FFT_N2, ct)

    def stage1(g, carry):
        rows = pl.ds(pl.multiple_of(g * FFT_SUB, FFT_SUB), FFT_SUB)
        x = xs_ref[:, rows, :].reshape(FFT_H1 * FFT_SUB, ct).astype(BF16)
        a = jnp.dot(m1_ref[...], x, preferred_element_type=F32)
        a_ref[:, rows, :] = a.reshape(FFT_R, FFT_SUB, ct)
        return carry

    lax.fori_loop(0, FFT_N2 // FFT_SUB, stage1, 0)

    def stage2(k1, carry):
        d = a_ref[pl.ds(2 * k1, 2)].reshape(2 * FFT_N2, ct).astype(BF16)
        emit(k1, jnp.dot(tf_ref[k1], d, preferred_element_type=F32))
        return carry

    lax.fori_loop(0, FFT_K1, stage2, 0, unroll=FFT_UNROLL)


def _filter_spectrum_kernel(hf_ref, hb_ref, m1_ref, tf_ref, o_ref, xs_ref, a_ref, f_ref):
    half = FFT_N2

    def keep(k1, x):
        f_ref[k1] = x

    _dft_forward(hf_ref, xs_ref, a_ref, m1_ref, tf_ref, keep)

    def combine(k1, x):
        f = f_ref[k1]
        o_ref[k1, 0:half, :] = (f[0:half] + x[0:half]).astype(BF16)
        o_ref[k1, half:, :] = (f[half:] - x[half:]).astype(BF16)

    _dft_forward(hb_ref, xs_ref, a_ref, m1_ref, tf_ref, combine)


def _long_conv_kernel(z_ref, hf_ref, m1_ref, tf_ref, ti_ref, m1i_ref, y_ref, xs_ref, a_ref, g_ref):
    half = FFT_N2
    ct = z_ref.shape[1]
    g_ref[2 * FFT_K1:] = jnp.zeros((FFT_R - 2 * FFT_K1, half, ct), F32)

    def product_and_invert(k1, x):
        hr = hf_ref[k1, 0:half, :].astype(F32)
        hi = hf_ref[k1, half:, :].astype(F32)
        xr, xi = x[0:half], x[half:]
        p = jnp.concatenate([xr * hr - xi * hi, xr * hi + xi * hr], axis=0).astype(BF16)
        g_ref[pl.ds(2 * k1, 2)] = jnp.dot(ti_ref[k1], p, preferred_element_type=F32).reshape(2, half, ct)

    _dft_forward(z_ref, xs_ref, a_ref, m1_ref, tf_ref, product_and_invert)

    def inverse_stage1(g, carry):
        rows = pl.ds(pl.multiple_of(g * FFT_SUB, FFT_SUB), FFT_SUB)
        x = g_ref[:, rows, :].reshape(FFT_R * FFT_SUB, ct).astype(BF16)
        y = jnp.dot(m1i_ref[...], x, preferred_element_type=F32)
        xs_ref[:, rows, :] = y.reshape(FFT_H1, FFT_SUB, ct)
        return carry

    lax.fori_loop(0, FFT_N2 // FFT_SUB, inverse_stage1, 0)
    y_ref[...] = xs_ref[...].reshape(L, ct).astype(y_ref.dtype)


def _long_conv_latent(z, filt):
    m1, m1i, tf, ti = _two_stage_dft()
    ct = FFT_CT
    n_ct = D // ct
    const = lambda shape: pl.BlockSpec(shape, lambda *_: (0,) * len(shape))
    scratch = [pltpu.VMEM((FFT_H1, FFT_N2, ct), F32), pltpu.VMEM((FFT_R, FFT_N2, ct), F32)]
    hf = pl.pallas_call(
        _filter_spectrum_kernel,
        grid=(n_ct,),
        in_specs=[
            pl.BlockSpec((L, ct), lambda c: (0, c)),
            pl.BlockSpec((L, ct), lambda c: (0, n_ct + c)),
            const(m1.shape), const(tf.shape),
        ],
        out_specs=pl.BlockSpec((FFT_K1, 2 * FFT_N2, ct), lambda c: (0, 0, c)),
        out_shape=jax.ShapeDtypeStruct((FFT_K1, 2 * FFT_N2, D), BF16),
        scratch_shapes=scratch + [pltpu.VMEM((FFT_K1, 2 * FFT_N2, ct), F32)],
        compiler_params=_cparams(("parallel",), 48),
        name="filter_spectrum",
    )(filt, filt, m1, tf)
    return pl.pallas_call(
        _long_conv_kernel,
        grid=(n_ct, B),
        in_specs=[
            pl.BlockSpec((L, ct), lambda c, b: (b, c)),
            pl.BlockSpec((FFT_K1, 2 * FFT_N2, ct), lambda c, b: (0, 0, c)),
            const(m1.shape), const(tf.shape), const(ti.shape), const(m1i.shape),
        ],
        out_specs=pl.BlockSpec((L, ct), lambda c, b: (b, c)),
        out_shape=jax.ShapeDtypeStruct((T, D), BF16),
        scratch_shapes=scratch + [pltpu.VMEM((FFT_R, FFT_N2, ct), F32)],
        compiler_params=_cparams(("parallel", "arbitrary"), 56),
        name="long_conv",
    )(z, hf, m1, tf, ti, m1i)


def _ctx_conv_kernel(z_ref, filt_ref, fwd_ref, inv_ref, _, y_ref):
    half = fwd_ref.shape[0] // 2
    xf = jnp.dot(fwd_ref[...], filt_ref[...], preferred_element_type=F32)
    hr = xf[0:half, 0:D] + xf[0:half, D:2 * D]
    hi = xf[half:, 0:D] - xf[half:, D:2 * D]
    x = jnp.dot(fwd_ref[...], z_ref[...], preferred_element_type=F32)
    xr, xi = x[0:half], x[half:]
    p = jnp.concatenate([xr * hr - xi * hi, xr * hi + xi * hr], axis=0).astype(BF16)
    y_ref[...] = jnp.dot(inv_ref[...], p, preferred_element_type=F32).astype(y_ref.dtype)


def _hyena_long_conv(z, filt_lat, filt_ctx):
    y = _long_conv_latent(z, filt_lat)
    fwd, inv = _direct_dft(CTX)
    full = lambda shape: pl.BlockSpec(shape, lambda b: (0,) * len(shape))
    ctx_rows = pl.BlockSpec((CTX, D), lambda b: (T_LAT // CTX + b, 0))
    return pl.pallas_call(
        _ctx_conv_kernel,
        grid=(B,),
        in_specs=[ctx_rows, full(filt_ctx.shape), full(fwd.shape), full(inv.shape),
                  pl.BlockSpec(memory_space=pl.ANY)],
        out_specs=ctx_rows,
        out_shape=jax.ShapeDtypeStruct((T, D), BF16),
        input_output_aliases={4: 0},
        compiler_params=_cparams(("parallel",), 48),
        name="ctx_conv",
    )(z, filt_ctx, fwd, inv, y)


def _layer_tail_kernel(*refs, hyena):
    h3, y3, refs = refs[0:3], refs[3:6], refs[6:]
    if hyena:
        z3, x03, skip_ref, refs = refs[0:3], refs[3:6], refs[6], refs[7:]
    mod_ref, g_ref, wo_ref, bo_ref, wup_ref, cw_ref, cb_ref, wd_ref, o_ref, xn_ref, gate_ref = refs
    i = pl.program_id(0)
    m = mod_ref[...]
    half = TM // 2
    ext_half = half + HALO

    def ext_rows(trio, s):
        prev_ref, main_ref, next_ref = trio[1], trio[0], trio[2]
        if s == 0:
            return jnp.concatenate([prev_ref[...], main_ref[0:half, :]], axis=0)
        return jnp.concatenate([main_ref[half:, :], next_ref[...]], axis=0)

    for s in range(2):
        mix = ext_rows(y3, s)
        if hyena:
            mix = ((mix.astype(F32) + ext_rows(z3, s).astype(F32) * skip_ref[...])
                   * ext_rows(x03, s).astype(F32)).astype(BF16)
        r = jnp.dot(mix, wo_ref[...], preferred_element_type=F32) + bo_ref[...]
        h_mid = ext_rows(h3, s) + m[:, 2 * D:3 * D] * _rms(r, g_ref[1:2, :])
        xn_ref[s * ext_half:(s + 1) * ext_half, :] = _norm_mod(
            h_mid, g_ref[2:3, :], m[:, 3 * D:4 * D], m[:, 4 * D:5 * D]).astype(BF16)
        o_ref[s * half:(s + 1) * half, :] = h_mid[HALO:, :] if s == 0 else h_mid[0:half, :]

    has_prev, has_next = _seg_masks(i)
    ext = SUB_ROWS + 2 * HALO
    n_sb = TM // SUB_ROWS
    units = [(c, sb) for c in range(N_FC) for sb in range(n_sb)]

    def up_matmuls(k):
        c, sb = units[k]
        r0 = sb * SUB_ROWS
        gate_ref[k % 2] = jnp.dot(xn_ref[r0:r0 + ext, :], wup_ref[:, c * FC:(c + 1) * FC],
                                  preferred_element_type=F32)
        return jnp.dot(xn_ref[HALO + r0:HALO + r0 + SUB_ROWS, :],
                       wup_ref[:, D_FF + c * FC:D_FF + (c + 1) * FC], preferred_element_type=F32)

    acc = [None] * n_sb
    up_next = up_matmuls(0)
    for k, (c, sb) in enumerate(units):
        up = up_next
        if k + 1 < len(units):
            up_next = up_matmuls(k + 1)
        cols = slice(c * FC, (c + 1) * FC)
        rows = slice(sb * SUB_ROWS, (sb + 1) * SUB_ROWS)
        conv = _conv3(gate_ref.at[k % 2], cw_ref[:, cols], cb_ref[:, cols], has_prev[rows], has_next[rows], SUB_ROWS)
        part = jnp.dot((_silu(conv) * up).astype(BF16), wd_ref[cols, :], preferred_element_type=F32)
        acc[sb] = part if c == 0 else acc[sb] + part
    for sb in range(n_sb):
        rows = slice(sb * SUB_ROWS, (sb + 1) * SUB_ROWS)
        o_ref[rows, :] = o_ref[rows, :] + m[:, 5 * D:6 * D] * _rms(acc[sb], g_ref[3:4, :])


def _layer_tail(h, y, mods_l, g, w_out, b_out, w_up, cw, cb, w_down, n_tiles, hyena_args=None):
    resident = lambda shape: pl.BlockSpec(shape, lambda i: (0,) * len(shape), pipeline_mode=pl.Buffered(1))
    hyena = hyena_args is not None
    in_specs = _row_specs(h.shape[0]) + _row_specs(y.shape[0])
    args = [h, h, h, y, y, y]
    if hyena:
        z, x0, skip = hyena_args
        in_specs += _row_specs(z.shape[0]) + _row_specs(x0.shape[0]) + [resident((1, D))]
        args += [z, z, z, x0, x0, x0, skip]
    in_specs += [
        pl.BlockSpec((None, 1, N_MOD * D), lambda i: (_mod_row(i), 0, 0)),
        pl.BlockSpec((4, D), lambda i: (0, 0)),
        resident((D, D)), resident((1, D)),
        resident((D, 2 * D_FF)), resident((3, D_FF)), resident((1, D_FF)), resident((D_FF, D)),
    ]
    args += [mods_l, g, w_out, b_out, w_up, cw, cb, w_down]
    return pl.pallas_call(
        functools.partial(_layer_tail_kernel, hyena=hyena),
        grid=(n_tiles,),
        in_specs=in_specs,
        out_specs=pl.BlockSpec((TM, D), lambda i: (i, 0)),
        out_shape=jax.ShapeDtypeStruct((n_tiles * TM, D), F32),
        scratch_shapes=[
            pltpu.VMEM((TM + 2 * HALO, D), BF16),
            pltpu.VMEM((2, SUB_ROWS + 2 * HALO, FC), F32),
        ],
        compiler_params=_cparams(("parallel",), 56),
        name="layer_tail",
    )(*args)


def _dup_heads(w):
    k0, k1 = w[..., 0:C_DH], w[..., C_DH:2 * C_DH]
    return jnp.concatenate([k0, k0, k1, k1], axis=-1)


def kernel(x, c, ctx, c_ctx, mod_w, mod_b, norm_g, ffn_w_up, ffn_conv_w, ffn_conv_b, ffn_w_down, a_w_qkv, a_q_gain, a_k_gain, a_w_out, b_w_in, b_b_in, b_conv_w, b_conv_b, b_f_w1, b_f_b1, b_f_w2, b_f_b2, b_f_w3, b_f_b3, b_f_w4, b_skip, b_w_out, b_b_out, c_w_qkv, c_b_qkv, c_sink, c_w_out, c_b_out):
    h = jnp.concatenate([x.reshape(T_LAT, D), ctx.reshape(T_CTX, D)], axis=0)
    cs = jnp.concatenate([c, c_ctx[None], jnp.zeros((8 - B - 1, D), F32)], axis=0)
    mods = _mods(cs, mod_w, mod_b).reshape(DEPTH, 8, 1, N_MOD * D)
    cos_a, sin_a = _rope_tables(A_DH)
    cos_c, sin_c = _rope_tables(C_DH)
    zeros_d = jnp.zeros((1, D), F32)

    for i in range(DEPTH):
        last = i == DEPTH - 1
        kind, j = i % 3, i // 3
        g = norm_g[i]
        mods_l = mods[i]
        n_tiles = N_LAT_TILES if last else N_TILES
        if kind == 0:
            qkv = _proj_attn(h, mods_l, g, a_w_qkv[j].astype(BF16), jnp.zeros((1, QKV_COLS), F32),
                             a_q_gain[j].reshape(1, A_DH), a_k_gain[j].reshape(1, A_DH), cos_a, sin_a, "a")
            y = _attn_a(qkv, n_tiles * TM)
            if not last:
                y = _attn_a(qkv, T, into=y)
            w_out, b_out, hyena_args = a_w_out[j], zeros_d, None
        elif kind == 1:
            x0, z = _proj_b(h, mods_l, g, b_w_in[j].astype(BF16), b_b_in[j].reshape(1, 3 * D),
                            b_conv_w[j], b_conv_b[j].reshape(1, 3 * D))
            fw = (b_f_w1[j], b_f_b1[j], b_f_w2[j], b_f_b2[j], b_f_w3[j], b_f_b3[j], b_f_w4[j])
            y = _hyena_long_conv(z, _hyena_filter(L, *fw), _hyena_filter(CTX, *fw))
            w_out, b_out = b_w_out[j], b_b_out[j].reshape(1, D)
            hyena_args = (z, x0, b_skip[j].reshape(1, D))
        else:
            qc = C_HEADS * C_DH
            kc = qc + C_KV * C_DH
            w, bias = c_w_qkv[j], c_b_qkv[j].reshape(1, -1)
            w = jnp.concatenate([w[:, :qc], _dup_heads(w[:, qc:kc]), _dup_heads(w[:, kc:])], axis=1)
            bias = jnp.concatenate([bias[:, :qc], _dup_heads(bias[:, qc:kc]), _dup_heads(bias[:, kc:])], axis=1)
            ones = jnp.ones((1, 128), F32)
            qkv = _proj_attn(h, mods_l, g, w.astype(BF16), bias, ones, ones, cos_c, sin_c, "c")
            sink_rows = jnp.broadcast_to((c_sink[j].astype(F32) * LOG2_E)[:, None, None],
                                         (C_HEADS, 8, 128)).reshape(C_KV, (C_HEADS // C_KV) * 8, 128)
            y = _attn_c(qkv, sink_rows)
            w_out, b_out, hyena_args = c_w_out[j], c_b_out[j].reshape(1, D), None
        h = _layer_tail(h, y, mods_l, g, w_out.astype(BF16), b_out, ffn_w_up[i].astype(BF16), ffn_conv_w[i],
                        ffn_conv_b[i].reshape(1, D_FF), ffn_w_down[i].astype(BF16), n_tiles, hyena_args)
    return h.reshape(B, L, D)
```

```python
import functools
import math

import numpy as np
import jax
import jax.numpy as jnp
from jax import lax
from jax.experimental import pallas as pl
from jax.experimental.pallas import tpu as pltpu

F32 = jnp.float32
BF16 = jnp.bfloat16

D = 1024
B = 4
L = 4096
CTX = 256
DEPTH = 4
N_MOD = 6
EPS = 1e-6
NEG_INF = -1e30
LOG2_E = math.log2(math.e)
GRID_W = 64
ROPE_THETA = 10000.0

A_HEADS, A_KV, A_DH = 8, 2, 128
C_HEADS, C_KV, C_DH = 16, 2, 64
QKV_COLS = 1536
PROJ_GROUP = 4
SUB_ROWS = 256
D_FF = 2816
FILTER_BANDS = 16
FILTER_EMB = 1 + 2 * FILTER_BANDS
FILTER_WIDTH = 64
FILTER_SIN_FREQ = 1.0
DECAY_TARGET = 1e-2
DECAY_FAST_PCT = 0.3
DECAY_SLOW_PCT = 1.5

T_LAT = B * L
T_CTX = B * CTX
T = T_LAT + T_CTX

TM = 512
HALO = 16
N_LAT_TILES = T_LAT // TM
N_TILES = T // TM
FC = 1408
N_FC = D_FF // FC
WINDOW = 128
TQ = 256
TQA = 256
TK = 2048
NQ_LAT = L // TQ
NQ_CTX = CTX // TQ

FFT_N = 2 * L
FFT_N1 = 64
FFT_N2 = FFT_N // FFT_N1
FFT_H1 = FFT_N1 // 2
FFT_K1 = FFT_H1 + 1
FFT_R = 72
FFT_CT = 256
FFT_SUB = 8
FFT_UNROLL = 11
FFT_UNROLL1 = 4

VMEM_MB = 1024 * 1024


def _cparams(sem, vmem_mb):
    return pltpu.CompilerParams(dimension_semantics=sem, vmem_limit_bytes=vmem_mb * VMEM_MB)


def _rope_tables(head_dim):
    rows = L // GRID_W
    row = np.repeat(np.arange(rows), GRID_W).astype(np.float64)
    col = np.tile(np.arange(GRID_W), rows).astype(np.float64)
    axis_dim = head_dim // 2
    inv = np.power(ROPE_THETA, -np.arange(0, axis_dim, 2, dtype=np.float64) / axis_dim)
    ang = np.concatenate([row[:, None] * inv[None], col[:, None] * inv[None]], axis=-1)
    cos, sin = np.cos(ang), np.sin(ang)
    reps = 128 // head_dim
    cos_t = np.tile(np.concatenate([cos, cos], axis=-1), (1, reps))
    sin_t = np.tile(np.concatenate([-sin, sin], axis=-1), (1, reps))
    cos_t = np.concatenate([cos_t, np.ones((TM, 128))], axis=0)
    sin_t = np.concatenate([sin_t, np.zeros((TM, 128))], axis=0)
    return jnp.asarray(cos_t, F32), jnp.asarray(sin_t, F32)


def _two_stage_dft():
    n1s, n2s, h1, k1n, n = FFT_N1, FFT_N2, FFT_H1, FFT_K1, FFT_N
    n1 = np.arange(h1)
    k1 = np.arange(k1n)
    ang = 2 * np.pi * np.outer(k1, n1) / n1s
    m1 = np.zeros((FFT_R, h1))
    m1[0:2 * k1n:2] = np.cos(ang)
    m1[1:2 * k1n:2] = -np.sin(ang)
    c = np.where((k1 == 0) | (k1 == h1), 1.0, 2.0)
    m1i = np.zeros((h1, FFT_R))
    m1i[:, 0:2 * k1n:2] = (c[None] / n) * np.cos(ang.T)
    m1i[:, 1:2 * k1n:2] = -(c[None] / n) * np.sin(ang.T)
    n2 = np.arange(n2s)
    k2 = np.arange(n2s)
    tf = np.zeros((k1n, 2 * n2s, 2 * n2s))
    ti = np.zeros((k1n, 2 * n2s, 2 * n2s))
    for a in range(k1n):
        th = 2 * np.pi * np.outer(a + n1s * k2, n2) / n
        er, ei = np.cos(th), -np.sin(th)
        tf[a] = np.block([[er, -ei], [ei, er]])
        er, ei = np.cos(th.T), np.sin(th.T)
        ti[a] = np.block([[er, -ei], [ei, er]])
    eye = np.eye(FFT_SUB)
    return tuple(jnp.asarray(a, F32).astype(BF16) for a in (np.kron(m1, eye), np.kron(m1i, eye), tf, ti))


def _direct_dft(seq_len):
    n = 2 * seq_len
    th = 2 * np.pi * np.outer(np.arange(n), np.arange(seq_len)) / n
    fwd = np.concatenate([np.cos(th), -np.sin(th)], axis=0)
    inv = np.concatenate([np.cos(th.T), -np.sin(th.T)], axis=1) / n
    return jnp.asarray(fwd, F32).astype(BF16), jnp.asarray(inv, F32).astype(BF16)


def _filter_features(seq_len):
    t = np.linspace(0.0, 1.0, seq_len)[:, None]
    omega = (2.0 * math.pi / seq_len) * np.arange(seq_len, dtype=np.float64)
    bands = np.linspace(1e-4, FILTER_BANDS - 1, FILTER_BANDS)
    ang = omega[:, None] * bands[None, :]
    z = np.concatenate([t, np.cos(ang), -np.sin(ang)], axis=-1)
    z = np.pad(z, ((0, 0), (0, FILTER_WIDTH - FILTER_EMB)))
    rates = np.abs(np.linspace(math.log(DECAY_TARGET) / DECAY_FAST_PCT,
                               math.log(DECAY_TARGET) / DECAY_SLOW_PCT, D))[None]
    return jnp.asarray(z, F32), jnp.asarray(t, F32), jnp.asarray(rates, F32)


def _rms(x, g):
    return x * lax.rsqrt(jnp.mean(x * x, axis=-1, keepdims=True) + EPS) * g


def _norm_mod(h, g, shift, scale):
    return _rms(h, g) * (1.0 + scale) + shift


def _silu(x):
    return x * (1.0 / (1.0 + jnp.exp(-x)))


def _seg_masks(i):
    rows = i * TM + lax.broadcasted_iota(jnp.int32, (TM, 1), 0)
    seg = jnp.where(i < N_LAT_TILES, L, CTX)
    pos = rows & (seg - 1)
    return (pos != 0).astype(F32), (pos != seg - 1).astype(F32)


def _fill_xn(xn_ref, h_ref, hp_ref, hn_ref, g, shift, scale):
    xn_ref[0:HALO, :] = _norm_mod(hp_ref[...], g, shift, scale).astype(BF16)
    xn_ref[HALO:HALO + TM, :] = _norm_mod(h_ref[...], g, shift, scale).astype(BF16)
    xn_ref[HALO + TM:, :] = _norm_mod(hn_ref[...], g, shift, scale).astype(BF16)


def _conv3(u_ref, cw, cb, has_prev, has_next, n_rows):
    u = u_ref[...]
    total = n_rows + 2 * HALO
    u_prev = pltpu.roll(u, 1, 0)
    u_next = pltpu.roll(u, total - 1, 0)

    def piece(r0, r1, masked):
        prev = u_prev[HALO + r0:HALO + r1]
        nxt = u_next[HALO + r0:HALO + r1]
        if masked:
            prev = prev * has_prev[r0:r1]
            nxt = nxt * has_next[r0:r1]
        return cb + cw[0:1] * prev + cw[1:2] * u[HALO + r0:HALO + r1] + cw[2:3] * nxt

    pieces = []
    for s in range(0, n_rows, CTX):
        pieces += [piece(s, s + 8, True), piece(s + 8, s + CTX - 8, False), piece(s + CTX - 8, s + CTX, True)]
    return jnp.concatenate(pieces, axis=0)


def _mod_row(i):
    return jnp.where(i < N_LAT_TILES, i // (L // TM), B)


def _row_specs(in_rows):
    last = in_rows // HALO - 1
    per = TM // HALO
    return [
        pl.BlockSpec((TM, D), lambda i, *_: (i, 0)),
        pl.BlockSpec((HALO, D), lambda i, *_: (jnp.maximum(i * per - 1, 0), 0)),
        pl.BlockSpec((HALO, D), lambda i, *_: (jnp.minimum((i + 1) * per, last), 0)),
    ]


def _mods_kernel(s_ref, w_ref, b_ref, o_ref):
    s = _silu(s_ref[...])
    o_ref[...] = jnp.dot(s, w_ref[...], preferred_element_type=F32,
                         precision=lax.Precision.HIGHEST) + b_ref[...]


def _mods(cs, mod_w, mod_b):
    tn = 1536
    return pl.pallas_call(
        _mods_kernel,
        grid=(DEPTH, N_MOD * D // tn),
        in_specs=[
            pl.BlockSpec((8, D), lambda l, n: (0, 0)),
            pl.BlockSpec((None, D, tn), lambda l, n: (l, 0, n)),
            pl.BlockSpec((None, 1, tn), lambda l, n: (l, 0, n)),
        ],
        out_specs=pl.BlockSpec((None, 8, tn), lambda l, n: (l, 0, n)),
        out_shape=jax.ShapeDtypeStruct((DEPTH, 8, N_MOD * D), F32),
        compiler_params=_cparams(("parallel", "parallel"), 40),
        name="mods",
    )(cs, mod_w, mod_b.reshape(DEPTH, 1, N_MOD * D))


def _proj_attn_kernel(h_ref, mod_ref, g_ref, w_ref, b_ref, qg_ref, kg_ref, cos_ref, sin_ref, o_ref, *, kind):
    m = mod_ref[...]
    n_q = 8
    n_rope = 10
    q_scale = (A_DH if kind == "a" else C_DH) ** -0.5 * LOG2_E
    if kind == "c":
        lane = lax.broadcasted_iota(jnp.int32, (SUB_ROWS, 128), 1)
        first_half = (lane & (C_DH - 1)) < C_DH // 2
    for r0 in range(0, TM, SUB_ROWS):
        rows = slice(r0, r0 + SUB_ROWS)
        xn = _norm_mod(h_ref[rows, :], g_ref[0:1, :], m[:, 0:D], m[:, D:2 * D]).astype(BF16)
        cos, sin = cos_ref[rows, :], sin_ref[rows, :]
        cos_q, sin_q = cos * q_scale, sin * q_scale
        for j in range(QKV_COLS // 128):
            if j % PROJ_GROUP == 0:
                wide = slice(j * 128, (j + PROJ_GROUP) * 128)
                y_wide = jnp.dot(xn, w_ref[:, wide], preferred_element_type=F32) + b_ref[:, wide]
            y = y_wide[:, (j % PROJ_GROUP) * 128:(j % PROJ_GROUP + 1) * 128]
            if j < n_rope:
                if kind == "a":
                    y = _rms(y, qg_ref[...] if j < n_q else kg_ref[...])
                    rot = pltpu.roll(y, 64, 1)
                else:
                    rot = jnp.where(first_half, pltpu.roll(y, 128 - C_DH // 2, 1), pltpu.roll(y, C_DH // 2, 1))
                y = y * cos_q + rot * sin_q if j < n_q else y * cos + rot * sin
            o_ref[rows, j * 128:(j + 1) * 128] = y.astype(BF16)


def _proj_attn(h, mods_l, g, w, b, q_gain, k_gain, cos_t, sin_t, kind):
    tpb = L // TM
    rope_idx = lambda i: (jnp.where(i < N_LAT_TILES, i % tpb, tpb), 0)
    return pl.pallas_call(
        functools.partial(_proj_attn_kernel, kind=kind),
        grid=(N_TILES,),
        in_specs=[
            pl.BlockSpec((TM, D), lambda i: (i, 0)),
            pl.BlockSpec((None, 1, N_MOD * D), lambda i: (_mod_row(i), 0, 0)),
            pl.BlockSpec((4, D), lambda i: (0, 0)),
            pl.BlockSpec((D, QKV_COLS), lambda i: (0, 0)),
            pl.BlockSpec((1, QKV_COLS), lambda i: (0, 0)),
            pl.BlockSpec((1, 128), lambda i: (0, 0)),
            pl.BlockSpec((1, 128), lambda i: (0, 0)),
            pl.BlockSpec((TM, 128), rope_idx),
            pl.BlockSpec((TM, 128), rope_idx),
        ],
        out_specs=pl.BlockSpec((TM, QKV_COLS), lambda i: (i, 0)),
        out_shape=jax.ShapeDtypeStruct((T, QKV_COLS), BF16),
        compiler_params=_cparams(("parallel",), 40),
        name="proj_" + kind,
    )(h, mods_l, g, w, b, q_gain, k_gain, cos_t, sin_t)


def _qk(q, k):
    return lax.dot_general(q, k, (((1,), (1,)), ((), ())), preferred_element_type=F32)


def _ones_ext(v):
    return jnp.concatenate([v, jnp.ones(v.shape, BF16)], axis=1)


def _attn_a_kernel(q_ref, *refs, latent, merge):
    if merge:
        octx_ref, o_ref = refs[-2:]
        refs = refs[:-2] + (o_ref,)
        is_copy_step = pl.program_id(2) == L // TQA

        @pl.when(is_copy_step)
        def _():
            o_ref[...] = octx_ref[...]

        @pl.when(jnp.logical_not(is_copy_step))
        def _():
            _attn_a_body(q_ref, refs, latent)
    else:
        _attn_a_body(q_ref, refs, latent)


def _attn_a_body(q_ref, refs, latent):
    if latent:
        kl_ref, vl_ref, kc_ref, vc_ref, o_ref = refs
        chunks = [[(kc_ref[...], vc_ref[...]), (kl_ref[0:TK, :], vl_ref[0:TK, :])]]
        chunks += [[(kl_ref[c * TK:(c + 1) * TK, :], vl_ref[c * TK:(c + 1) * TK, :])] for c in range(1, L // TK)]
    else:
        kc_ref, vc_ref, o_ref = refs
        chunks = [[(kc_ref[...], vc_ref[...])]]
    g = A_HEADS // A_KV
    m = [None] * g
    acc = [None] * g
    for ci, pieces in enumerate(chunks):
        v_exts = [_ones_ext(v) for _, v in pieces]
        scores = [jnp.concatenate([_qk(q_ref[:, i * A_DH:(i + 1) * A_DH], k) for k, _ in pieces], axis=1)
                  for i in range(g)]
        for i in range(g):
            s = scores[i]
            m_cur = jnp.broadcast_to(jnp.max(s, axis=-1, keepdims=True), (TQA, 128))
            m_new = m_cur if ci == 0 else jnp.maximum(m[i], m_cur)
            p = jnp.exp2(s - jnp.tile(m_new, (1, s.shape[1] // 128))).astype(BF16)
            pv, off = None, 0
            for (k, _), v_ext in zip(pieces, v_exts):
                part = jnp.dot(p[:, off:off + k.shape[0]], v_ext, preferred_element_type=F32)
                pv = part if pv is None else pv + part
                off += k.shape[0]
            if ci == 0:
                acc[i] = pv
            else:
                acc[i] = jnp.tile(jnp.exp2(m[i] - m_new), (1, 2)) * acc[i] + pv
            m[i] = m_new
    for i in range(g):
        o_ref[:, i * A_DH:(i + 1) * A_DH] = (acc[i][:, 0:A_DH] / acc[i][:, A_DH:]).astype(BF16)


def _attn_a(qkv, latent, o_ctx=None):
    assert CTX == TQA
    nq = (L if latent else CTX) // TQA
    q0 = 0 if latent else T_LAT // TQA
    merge = o_ctx is not None
    gcols = (A_HEADS // A_KV) * A_DH // 128
    blk = (TQA, gcols * 128)
    in_specs = [pl.BlockSpec(blk, lambda b, kv, j: (q0 + b * nq + jnp.minimum(j, nq - 1), kv))]
    if latent:
        in_specs += [
            pl.BlockSpec((L, A_DH), lambda b, kv, j: (b, A_HEADS + kv)),
            pl.BlockSpec((L, A_DH), lambda b, kv, j: (b, A_HEADS + A_KV + kv)),
        ]
    in_specs += [
        pl.BlockSpec((CTX, A_DH), lambda b, kv, j: (T_LAT // CTX + b, A_HEADS + kv)),
        pl.BlockSpec((CTX, A_DH), lambda b, kv, j: (T_LAT // CTX + b, A_HEADS + A_KV + kv)),
    ]
    args = [qkv] * len(in_specs)
    if merge:
        in_specs.append(pl.BlockSpec(blk, lambda b, kv, j: (b, kv)))
        args.append(o_ctx)
        out_idx = lambda b, kv, j: (jnp.where(j < nq, b * nq + j, T_LAT // TQA + b), kv)
    else:
        out_idx = lambda b, kv, j: (b * nq + j, kv)
    return pl.pallas_call(
        functools.partial(_attn_a_kernel, latent=latent, merge=merge),
        grid=(B, A_KV, nq + merge),
        in_specs=in_specs,
        out_specs=pl.BlockSpec(blk, out_idx),
        out_shape=jax.ShapeDtypeStruct((T if merge else B * nq * TQA, D), BF16),
        compiler_params=_cparams(("parallel", "parallel", "arbitrary"), 48),
        name="attn_a_lat" if latent else "attn_a_ctx",
    )(*args)


def _attn_c_kernel(q_ref, kl_ref, vl_ref, kx_ref, vx_ref, sink_ref, o_ref):
    j = pl.program_id(2)
    g = C_HEADS // C_KV
    lane = lax.broadcasted_iota(jnp.int32, (TQ, 128), 1)
    lo = lane < C_DH
    zero = jnp.zeros((TQ, 128), BF16)
    is_lat = j < NQ_LAT
    q0 = j * TQ
    win = TQ + 2 * WINDOW
    s0 = pl.multiple_of(jnp.clip(q0 - WINDOW, 0, L - win), 128)
    k_all = jnp.concatenate([kx_ref[...], kl_ref[pl.ds(s0, win), :]], axis=0)
    v_ext = _ones_ext(jnp.concatenate([vx_ref[...], vl_ref[pl.ds(s0, win), :]], axis=0))
    rel = (lax.broadcasted_iota(jnp.int32, (TQ, win), 1) - lax.broadcasted_iota(jnp.int32, (TQ, win), 0)
           + jnp.where(is_lat, s0 - q0, 4 * win))
    bias = jnp.concatenate([jnp.zeros((TQ, CTX), F32), jnp.where(jnp.abs(rel) <= WINDOW, 0.0, NEG_INF)], axis=1)
    heads = []
    for p in range(g // 2):
        qb = q_ref[:, p * 128:(p + 1) * 128]
        heads += [jnp.where(lo, qb, zero), jnp.where(lo, zero, qb)]
    scores = [_qk(qh, k_all) + bias for qh in heads]
    outs = []
    for h in range(g):
        s = scores[h]
        sink = jnp.tile(sink_ref[h * 8:(h + 1) * 8, :], (TQ // 8, 1))
        m_part = sink
        for c in range(s.shape[1] // 128):
            m_part = jnp.maximum(m_part, s[:, c * 128:(c + 1) * 128])
        m = jnp.broadcast_to(jnp.max(m_part, axis=-1, keepdims=True), (TQ, 128))
        p = jnp.exp2(s - jnp.tile(m, (1, s.shape[1] // 128))).astype(BF16)
        pv = jnp.dot(p, v_ext, preferred_element_type=F32)
        outs.append(pv[:, 0:128] / (pv[:, 128:] + jnp.exp2(sink - m)))
    for p in range(g // 2):
        o_ref[:, p * 128:(p + 1) * 128] = jnp.where(lo, outs[2 * p], outs[2 * p + 1]).astype(BF16)


def _attn_c(qkv, sink_rows):
    nq = NQ_LAT + NQ_CTX
    qcols = C_HEADS * C_DH // C_KV
    kcol = C_HEADS * C_DH // 128
    vcol = kcol + C_KV
    n_lat_blocks = T_LAT // TQ

    def q_idx(b, kv, j):
        return (jnp.where(j < NQ_LAT, b * NQ_LAT + j, n_lat_blocks + b * NQ_CTX + (j - NQ_LAT)), kv)

    lat = lambda col: pl.BlockSpec((L, 128), lambda b, kv, j: (b, col + kv))
    ctx = lambda col: pl.BlockSpec((CTX, 128), lambda b, kv, j: (T_LAT // CTX + b, col + kv))
    return pl.pallas_call(
        _attn_c_kernel,
        grid=(B, C_KV, nq),
        in_specs=[
            pl.BlockSpec((TQ, qcols), q_idx),
            lat(kcol), lat(vcol), ctx(kcol), ctx(vcol),
            pl.BlockSpec((None, (C_HEADS // C_KV) * 8, 128), lambda b, kv, j: (kv, 0, 0)),
        ],
        out_specs=pl.BlockSpec((TQ, qcols), q_idx),
        out_shape=jax.ShapeDtypeStruct((T, D), BF16),
        compiler_params=_cparams(("parallel", "parallel", "arbitrary"), 48),
        name="attn_c",
    )(qkv, qkv, qkv, qkv, qkv, sink_rows)


def _proj_b_kernel(h_ref, hp_ref, hn_ref, mod_ref, g_ref, w_ref, b_ref, cw_ref, cb_ref,
                   x0_ref, z_ref, xn_ref, u_ref):
    i = pl.program_id(0)
    m = mod_ref[...]
    _fill_xn(xn_ref, h_ref, hp_ref, hn_ref, g_ref[0:1, :], m[:, 0:D], m[:, D:2 * D])
    has_prev, has_next = _seg_masks(i)

    def matmul(k):
        u_ref[k % 2] = (jnp.dot(xn_ref[...], w_ref[:, k * D:(k + 1) * D], preferred_element_type=F32)
                        + b_ref[:, k * D:(k + 1) * D])

    matmul(0)
    x1 = None
    for k in range(3):
        if k + 1 < 3:
            matmul(k + 1)
        cols = slice(k * D, (k + 1) * D)
        conv = _conv3(u_ref.at[k % 2], cw_ref[:, cols], cb_ref[:, cols], has_prev, has_next, TM)
        if k == 0:
            x0_ref[...] = conv.astype(BF16)
        elif k == 1:
            x1 = conv
        else:
            z_ref[...] = (conv * x1).astype(BF16)


def _proj_b(h, mods_l, g, w, b, cw, cb):
    full = lambda shape: pl.BlockSpec(shape, lambda i: (0,) * len(shape))
    return pl.pallas_call(
        _proj_b_kernel,
        grid=(N_TILES,),
        in_specs=_row_specs(T) + [
            pl.BlockSpec((None, 1, N_MOD * D), lambda i: (_mod_row(i), 0, 0)),
            full((4, D)), full((D, 3 * D)), full((1, 3 * D)), full((3, 3 * D)), full((1, 3 * D)),
        ],
        out_specs=[pl.BlockSpec((TM, D), lambda i: (i, 0))] * 2,
        out_shape=[jax.ShapeDtypeStruct((T, D), BF16)] * 2,
        scratch_shapes=[pltpu.VMEM((TM + 2 * HALO, D), BF16), pltpu.VMEM((2, TM + 2 * HALO, D), F32)],
        compiler_params=_cparams(("parallel",), 48),
        name="proj_b",
    )(h, h, h, mods_l, g, w, b, cw, cb)


def _filter_kernel(zf_ref, t_ref, w1_ref, b1_ref, w2_ref, b2_ref, w3_ref, b3_ref, w4_ref, rates_ref, o_ref):
    i = pl.program_id(0)
    tm = zf_ref.shape[0]
    dot = functools.partial(jnp.dot, preferred_element_type=F32, precision=lax.Precision.HIGHEST)
    f = jnp.sin(FILTER_SIN_FREQ * (dot(zf_ref[...], w1_ref[...]) + b1_ref[...]))
    f = jnp.sin(FILTER_SIN_FREQ * (dot(f, w2_ref[...]) + b2_ref[...]))
    f = jnp.sin(FILTER_SIN_FREQ * (dot(f, w3_ref[...]) + b3_ref[...]))
    f = jnp.dot(f.astype(BF16), w4_ref[...].astype(BF16), preferred_element_type=F32)
    decay = jnp.exp(-t_ref[...] * rates_ref[...])
    row = i * tm + lax.broadcasted_iota(jnp.int32, (tm, 1), 0)
    o_ref[:, 0:D] = (f[:, 0:D] * decay).astype(BF16)
    o_ref[:, D:2 * D] = (f[:, D:2 * D] * decay * (row != 0).astype(F32)).astype(BF16)


def _hyena_filter(seq_len, w1, b1, w2, b2, w3, b3, w4):
    zf, t, rates = _filter_features(seq_len)
    tm = min(seq_len, 512)
    w1p = jnp.pad(w1, ((0, FILTER_WIDTH - FILTER_EMB), (0, 0)))
    full = lambda shape: pl.BlockSpec(shape, lambda i: (0,) * len(shape))
    fw = FILTER_WIDTH
    return pl.pallas_call(
        _filter_kernel,
        grid=(seq_len // tm,),
        in_specs=[
            pl.BlockSpec((tm, fw), lambda i: (i, 0)),
            pl.BlockSpec((tm, 1), lambda i: (i, 0)),
            full((fw, fw)), full((1, fw)), full((fw, fw)), full((1, fw)), full((fw, fw)), full((1, fw)),
            full((fw, 2 * D)), full((1, D)),
        ],
        out_specs=pl.BlockSpec((tm, 2 * D), lambda i: (i, 0)),
        out_shape=jax.ShapeDtypeStruct((seq_len, 2 * D), BF16),
        compiler_params=_cparams(("parallel",), 40),
        name="hyena_filter",
    )(zf, t, w1p, b1.reshape(1, fw), w2, b2.reshape(1, fw), w3, b3.reshape(1, fw), w4, rates)


def _dft_forward(x_ref, xs_ref, a_ref, m1_ref, tf_ref, emit):
    ct = x_ref.shape[1]
    xs_ref[...] = x_ref[...].astype(F32).reshape(FFT_H1, FFT_N2, ct)

    def stage1(g, carry):
        rows = pl.ds(pl.multiple_of(g * FFT_SUB, FFT_SUB), FFT_SUB)
        x = xs_ref[:, rows, :].reshape(FFT_H1 * FFT_SUB, ct).astype(BF16)
        a = jnp.dot(m1_ref[...], x, preferred_element_type=F32)
        a_ref[:, rows, :] = a.reshape(FFT_R, FFT_SUB, ct)
        return carry

    lax.fori_loop(0, FFT_N2 // FFT_SUB, stage1, 0, unroll=FFT_UNROLL1)

    def stage2(k1, carry):
        d = a_ref[pl.ds(2 * k1, 2)].reshape(2 * FFT_N2, ct).astype(BF16)
        emit(k1, jnp.dot(tf_ref[k1], d, preferred_element_type=F32))
        return carry

    lax.fori_loop(0, FFT_K1, stage2, 0, unroll=FFT_UNROLL)


def _filter_spectrum_kernel(hf_ref, hb_ref, m1_ref, tf_ref, o_ref, xs_ref, a_ref, f_ref):
    half = FFT_N2

    def keep(k1, x):
        f_ref[k1] = x

    _dft_forward(hf_ref, xs_ref, a_ref, m1_ref, tf_ref, keep)

    def combine(k1, x):
        f = f_ref[k1]
        o_ref[k1, 0:half, :] = (f[0:half] + x[0:half]).astype(BF16)
        o_ref[k1, half:, :] = (f[half:] - x[half:]).astype(BF16)

    _dft_forward(hb_ref, xs_ref, a_ref, m1_ref, tf_ref, combine)


def _long_conv_kernel(z_ref, hf_ref, m1_ref, tf_ref, ti_ref, m1i_ref, _, y_ref, xs_ref, a_ref, g_ref):
    half = FFT_N2
    ct = z_ref.shape[1]
    g_ref[2 * FFT_K1:] = jnp.zeros((FFT_R - 2 * FFT_K1, half, ct), F32)

    def product_and_invert(k1, x):
        hr = hf_ref[k1, 0:half, :].astype(F32)
        hi = hf_ref[k1, half:, :].astype(F32)
        xr, xi = x[0:half], x[half:]
        p = jnp.concatenate([xr * hr - xi * hi, xr * hi + xi * hr], axis=0).astype(BF16)
        g_ref[pl.ds(2 * k1, 2)] = jnp.dot(ti_ref[k1], p, preferred_element_type=F32).reshape(2, half, ct)

    _dft_forward(z_ref, xs_ref, a_ref, m1_ref, tf_ref, product_and_invert)

    def inverse_stage1(g, carry):
        rows = pl.ds(pl.multiple_of(g * FFT_SUB, FFT_SUB), FFT_SUB)
        x = g_ref[:, rows, :].reshape(FFT_R * FFT_SUB, ct).astype(BF16)
        y = jnp.dot(m1i_ref[...], x, preferred_element_type=F32)
        xs_ref[:, rows, :] = y.reshape(FFT_H1, FFT_SUB, ct)
        return carry

    lax.fori_loop(0, FFT_N2 // FFT_SUB, inverse_stage1, 0, unroll=FFT_UNROLL1)
    y_ref[...] = xs_ref[...].reshape(L, ct).astype(y_ref.dtype)


def _long_conv_latent(z, filt, spare):
    m1, m1i, tf, ti = _two_stage_dft()
    ct = FFT_CT
    n_ct = D // ct
    const = lambda shape: pl.BlockSpec(shape, lambda *_: (0,) * len(shape))
    scratch = [pltpu.VMEM((FFT_H1, FFT_N2, ct), F32), pltpu.VMEM((FFT_R, FFT_N2, ct), F32)]
    hf = pl.pallas_call(
        _filter_spectrum_kernel,
        grid=(n_ct,),
        in_specs=[
            pl.BlockSpec((L, ct), lambda c: (0, c)),
            pl.BlockSpec((L, ct), lambda c: (0, n_ct + c)),
            const(m1.shape), const(tf.shape),
        ],
        out_specs=pl.BlockSpec((FFT_K1, 2 * FFT_N2, ct), lambda c: (0, 0, c)),
        out_shape=jax.ShapeDtypeStruct((FFT_K1, 2 * FFT_N2, D), BF16),
        scratch_shapes=scratch + [pltpu.VMEM((FFT_K1, 2 * FFT_N2, ct), F32)],
        compiler_params=_cparams(("parallel",), 48),
        name="filter_spectrum",
    )(filt, filt, m1, tf)
    return pl.pallas_call(
        _long_conv_kernel,
        grid=(n_ct, B),
        in_specs=[
            pl.BlockSpec((L, ct), lambda c, b: (b, c)),
            pl.BlockSpec((FFT_K1, 2 * FFT_N2, ct), lambda c, b: (0, 0, c)),
            const(m1.shape), const(tf.shape), const(ti.shape), const(m1i.shape),
            pl.BlockSpec(memory_space=pl.ANY),
        ],
        out_specs=pl.BlockSpec((L, ct), lambda c, b: (b, c)),
        out_shape=jax.ShapeDtypeStruct((T, D), BF16),
        input_output_aliases={6: 0},
        scratch_shapes=scratch + [pltpu.VMEM((FFT_R, FFT_N2, ct), F32)],
        compiler_params=_cparams(("parallel", "arbitrary"), 56),
        name="long_conv",
    )(z, hf, m1, tf, ti, m1i, spare)


def _ctx_conv_kernel(z_ref, filt_ref, fwd_ref, inv_ref, _, y_ref):
    half = fwd_ref.shape[0] // 2
    xf = jnp.dot(fwd_ref[...], filt_ref[...], preferred_element_type=F32)
    hr = xf[0:half, 0:D] + xf[0:half, D:2 * D]
    hi = xf[half:, 0:D] - xf[half:, D:2 * D]
    x = jnp.dot(fwd_ref[...], z_ref[...], preferred_element_type=F32)
    xr, xi = x[0:half], x[half:]
    p = jnp.concatenate([xr * hr - xi * hi, xr * hi + xi * hr], axis=0).astype(BF16)
    y_ref[...] = jnp.dot(inv_ref[...], p, preferred_element_type=F32).astype(y_ref.dtype)


def _hyena_long_conv(z, filt_lat, filt_ctx, spare):
    y = _long_conv_latent(z, filt_lat, spare)
    fwd, inv = _direct_dft(CTX)
    full = lambda shape: pl.BlockSpec(shape, lambda b: (0,) * len(shape))
    ctx_rows = pl.BlockSpec((CTX, D), lambda b: (T_LAT // CTX + b, 0))
    return pl.pallas_call(
        _ctx_conv_kernel,
        grid=(B,),
        in_specs=[ctx_rows, full(filt_ctx.shape), full(fwd.shape), full(inv.shape),
                  pl.BlockSpec(memory_space=pl.ANY)],
        out_specs=ctx_rows,
        out_shape=jax.ShapeDtypeStruct((T, D), BF16),
        input_output_aliases={4: 0},
        compiler_params=_cparams(("parallel",), 48),
        name="ctx_conv",
    )(z, filt_ctx, fwd, inv, y)


def _layer_tail_kernel(*refs, hyena):
    h3, y3, refs = refs[0:3], refs[3:6], refs[6:]
    if hyena:
        z3, x03, skip_ref, refs = refs[0:3], refs[3:6], refs[6], refs[7:]
    mod_ref, g_ref, wo_ref, bo_ref, wup_ref, cw_ref, cb_ref, wd_ref, o_ref, xn_ref, gate_ref = refs
    i = pl.program_id(0)
    m = mod_ref[...]
    half = TM // 2
    ext_half = half + HALO

    def ext_rows(trio, s):
        prev_ref, main_ref, next_ref = trio[1], trio[0], trio[2]
        if s == 0:
            return jnp.concatenate([prev_ref[...], main_ref[0:half, :]], axis=0)
        return jnp.concatenate([main_ref[half:, :], next_ref[...]], axis=0)

    for s in range(2):
        mix = ext_rows(y3, s)
        if hyena:
            mix = ((mix.astype(F32) + ext_rows(z3, s).astype(F32) * skip_ref[...])
                   * ext_rows(x03, s).astype(F32)).astype(BF16)
        r = jnp.dot(mix, wo_ref[...], preferred_element_type=F32) + bo_ref[...]
        h_mid = ext_rows(h3, s) + m[:, 2 * D:3 * D] * _rms(r, g_ref[1:2, :])
        xn_ref[s * ext_half:(s + 1) * ext_half, :] = _norm_mod(
            h_mid, g_ref[2:3, :], m[:, 3 * D:4 * D], m[:, 4 * D:5 * D]).astype(BF16)
        o_ref[s * half:(s + 1) * half, :] = h_mid[HALO:, :] if s == 0 else h_mid[0:half, :]

    has_prev, has_next = _seg_masks(i)
    ext = SUB_ROWS + 2 * HALO
    n_sb = TM // SUB_ROWS
    units = [(c, sb) for c in range(N_FC) for sb in range(n_sb)]

    def up_matmuls(k):
        c, sb = units[k]
        r0 = sb * SUB_ROWS
        gate_ref[k % 2] = jnp.dot(xn_ref[r0:r0 + ext, :], wup_ref[:, c * FC:(c + 1) * FC],
                                  preferred_element_type=F32)
        return jnp.dot(xn_ref[HALO + r0:HALO + r0 + SUB_ROWS, :],
                       wup_ref[:, D_FF + c * FC:D_FF + (c + 1) * FC], preferred_element_type=F32)

    acc = [None] * n_sb
    up_next = up_matmuls(0)
    for k, (c, sb) in enumerate(units):
        up = up_next
        if k + 1 < len(units):
            up_next = up_matmuls(k + 1)
        cols = slice(c * FC, (c + 1) * FC)
        rows = slice(sb * SUB_ROWS, (sb + 1) * SUB_ROWS)
        conv = _conv3(gate_ref.at[k % 2], cw_ref[:, cols], cb_ref[:, cols], has_prev[rows], has_next[rows], SUB_ROWS)
        part = jnp.dot((_silu(conv) * up).astype(BF16), wd_ref[cols, :], preferred_element_type=F32)
        acc[sb] = part if c == 0 else acc[sb] + part
    for sb in range(n_sb):
        rows = slice(sb * SUB_ROWS, (sb + 1) * SUB_ROWS)
        o_ref[rows, :] = o_ref[rows, :] + m[:, 5 * D:6 * D] * _rms(acc[sb], g_ref[3:4, :])


def _layer_tail(h, y, mods_l, g, w_out, b_out, w_up, cw, cb, w_down, n_tiles, hyena_args=None):
    resident = lambda shape: pl.BlockSpec(shape, lambda i: (0,) * len(shape), pipeline_mode=pl.Buffered(1))
    hyena = hyena_args is not None
    in_specs = _row_specs(h.shape[0]) + _row_specs(y.shape[0])
    args = [h, h, h, y, y, y]
    if hyena:
        z, x0, skip = hyena_args
        in_specs += _row_specs(z.shape[0]) + _row_specs(x0.shape[0]) + [resident((1, D))]
        args += [z, z, z, x0, x0, x0, skip]
    in_specs += [
        pl.BlockSpec((None, 1, N_MOD * D), lambda i: (_mod_row(i), 0, 0)),
        pl.BlockSpec((4, D), lambda i: (0, 0)),
        resident((D, D)), resident((1, D)),
        resident((D, 2 * D_FF)), resident((3, D_FF)), resident((1, D_FF)), resident((D_FF, D)),
    ]
    args += [mods_l, g, w_out, b_out, w_up, cw, cb, w_down]
    return pl.pallas_call(
        functools.partial(_layer_tail_kernel, hyena=hyena),
        grid=(n_tiles,),
        in_specs=in_specs,
        out_specs=pl.BlockSpec((TM, D), lambda i: (i, 0)),
        out_shape=jax.ShapeDtypeStruct((n_tiles * TM, D), F32),
        scratch_shapes=[
            pltpu.VMEM((TM + 2 * HALO, D), BF16),
            pltpu.VMEM((2, SUB_ROWS + 2 * HALO, FC), F32),
        ],
        compiler_params=_cparams(("parallel",), 56),
        name="layer_tail",
    )(*args)


def _dup_heads(w):
    k0, k1 = w[..., 0:C_DH], w[..., C_DH:2 * C_DH]
    return jnp.concatenate([k0, k0, k1, k1], axis=-1)


def kernel(x, c, ctx, c_ctx, mod_w, mod_b, norm_g, ffn_w_up, ffn_conv_w, ffn_conv_b, ffn_w_down, a_w_qkv, a_q_gain, a_k_gain, a_w_out, b_w_in, b_b_in, b_conv_w, b_conv_b, b_f_w1, b_f_b1, b_f_w2, b_f_b2, b_f_w3, b_f_b3, b_f_w4, b_skip, b_w_out, b_b_out, c_w_qkv, c_b_qkv, c_sink, c_w_out, c_b_out):
    h = jnp.concatenate([x.reshape(T_LAT, D), ctx.reshape(T_CTX, D)], axis=0)
    cs = jnp.concatenate([c, c_ctx[None], jnp.zeros((8 - B - 1, D), F32)], axis=0)
    mods = _mods(cs, mod_w, mod_b).reshape(DEPTH, 8, 1, N_MOD * D)
    cos_a, sin_a = _rope_tables(A_DH)
    cos_c, sin_c = _rope_tables(C_DH)
    zeros_d = jnp.zeros((1, D), F32)
    y = None

    for i in range(DEPTH):
        last = i == DEPTH - 1
        kind, j = i % 3, i // 3
        g = norm_g[i]
        mods_l = mods[i]
        n_tiles = N_LAT_TILES if last else N_TILES
        if kind == 0:
            qkv = _proj_attn(h, mods_l, g, a_w_qkv[j].astype(BF16), jnp.zeros((1, QKV_COLS), F32),
                             a_q_gain[j].reshape(1, A_DH), a_k_gain[j].reshape(1, A_DH), cos_a, sin_a, "a")
            y = _attn_a(qkv, True, None if last else _attn_a(qkv, False))
            w_out, b_out, hyena_args = a_w_out[j], zeros_d, None
        elif kind == 1:
            x0, z = _proj_b(h, mods_l, g, b_w_in[j].astype(BF16), b_b_in[j].reshape(1, 3 * D),
                            b_conv_w[j], b_conv_b[j].reshape(1, 3 * D))
            fw = (b_f_w1[j], b_f_b1[j], b_f_w2[j], b_f_b2[j], b_f_w3[j], b_f_b3[j], b_f_w4[j])
            spare = y if y is not None and y.shape == (T, D) else jnp.zeros((T, D), BF16)
            y = _hyena_long_conv(z, _hyena_filter(L, *fw), _hyena_filter(CTX, *fw), spare)
            w_out, b_out = b_w_out[j], b_b_out[j].reshape(1, D)
            hyena_args = (z, x0, b_skip[j].reshape(1, D))
        else:
            qc = C_HEADS * C_DH
            kc = qc + C_KV * C_DH
            w, bias = c_w_qkv[j], c_b_qkv[j].reshape(1, -1)
            w = jnp.concatenate([w[:, :qc], _dup_heads(w[:, qc:kc]), _dup_heads(w[:, kc:])], axis=1)
            bias = jnp.concatenate([bias[:, :qc], _dup_heads(bias[:, qc:kc]), _dup_heads(bias[:, kc:])], axis=1)
            ones = jnp.ones((1, 128), F32)
            qkv = _proj_attn(h, mods_l, g, w.astype(BF16), bias, ones, ones, cos_c, sin_c, "c")
            sink_rows = jnp.broadcast_to((c_sink[j].astype(F32) * LOG2_E)[:, None, None],
                                         (C_HEADS, 8, 128)).reshape(C_KV, (C_HEADS // C_KV) * 8, 128)
            y = _attn_c(qkv, sink_rows)
            w_out, b_out, hyena_args = c_w_out[j], c_b_out[j].reshape(1, D), None
        h = _layer_tail(h, y, mods_l, g, w_out.astype(BF16), b_out, ffn_w_up[i].astype(BF16), ffn_conv_w[i],
                        ffn_conv_b[i].reshape(1, D_FF), ffn_w_down[i].astype(BF16), n_tiles, hyena_args)
    return h.reshape(B, L, D)
```

```python
import functools
import math

import numpy as np
import jax
import jax.numpy as jnp
from jax import lax
from jax.experimental import pallas as pl
from jax.experimental.pallas import tpu as pltpu

F32 = jnp.float32
BF16 = jnp.bfloat16

D = 1024
B = 4
L = 4096
CTX = 256
DEPTH = 4
N_MOD = 6
EPS = 1e-6
NEG_INF = -1e30
LOG2_E = math.log2(math.e)
GRID_W = 64
ROPE_THETA = 10000.0

A_HEADS, A_KV, A_DH = 8, 2, 128
C_HEADS, C_KV, C_DH = 16, 2, 64
QKV_COLS = 1536
PROJ_GROUP = 4
SUB_ROWS = 256
D_FF = 2816
FILTER_BANDS = 16
FILTER_EMB = 1 + 2 * FILTER_BANDS
FILTER_WIDTH = 64
FILTER_SIN_FREQ = 1.0
DECAY_TARGET = 1e-2
DECAY_FAST_PCT = 0.3
DECAY_SLOW_PCT = 1.5

T_LAT = B * L
T_CTX = B * CTX
T = T_LAT + T_CTX

TM = 512
HALO = 16
N_LAT_TILES = T_LAT // TM
N_TILES = T // TM
FC = 1408
N_FC = D_FF // FC
WINDOW = 128
TQ = 256
TQA = 256
TK = 4096
NQ_LAT = L // TQ
NQ_CTX = CTX // TQ

FFT_N = 2 * L
FFT_N1 = 64
FFT_N2 = FFT_N // FFT_N1
FFT_H1 = FFT_N1 // 2
FFT_K1 = FFT_H1 + 1
FFT_R = 72
FFT_CT = 256
FFT_SUB = 8
FFT_UNROLL = 11
FFT_UNROLL1 = 4

VMEM_MB = 1024 * 1024


def _cparams(sem, vmem_mb):
    return pltpu.CompilerParams(dimension_semantics=sem, vmem_limit_bytes=vmem_mb * VMEM_MB)


def _rope_tables(head_dim):
    rows = L // GRID_W
    row = np.repeat(np.arange(rows), GRID_W).astype(np.float64)
    col = np.tile(np.arange(GRID_W), rows).astype(np.float64)
    axis_dim = head_dim // 2
    inv = np.power(ROPE_THETA, -np.arange(0, axis_dim, 2, dtype=np.float64) / axis_dim)
    ang = np.concatenate([row[:, None] * inv[None], col[:, None] * inv[None]], axis=-1)
    cos, sin = np.cos(ang), np.sin(ang)
    reps = 128 // head_dim
    cos_t = np.tile(np.concatenate([cos, cos], axis=-1), (1, reps))
    sin_t = np.tile(np.concatenate([-sin, sin], axis=-1), (1, reps))
    cos_t = np.concatenate([cos_t, np.ones((TM, 128))], axis=0)
    sin_t = np.concatenate([sin_t, np.zeros((TM, 128))], axis=0)
    return jnp.asarray(cos_t, F32), jnp.asarray(sin_t, F32)


def _two_stage_dft():
    n1s, n2s, h1, k1n, n = FFT_N1, FFT_N2, FFT_H1, FFT_K1, FFT_N
    n1 = np.arange(h1)
    k1 = np.arange(k1n)
    ang = 2 * np.pi * np.outer(k1, n1) / n1s
    m1 = np.zeros((FFT_R, h1))
    m1[0:2 * k1n:2] = np.cos(ang)
    m1[1:2 * k1n:2] = -np.sin(ang)
    c = np.where((k1 == 0) | (k1 == h1), 1.0, 2.0)
    m1i = np.zeros((h1, FFT_R))
    m1i[:, 0:2 * k1n:2] = (c[None] / n) * np.cos(ang.T)
    m1i[:, 1:2 * k1n:2] = -(c[None] / n) * np.sin(ang.T)
    n2 = np.arange(n2s)
    k2 = np.arange(n2s)
    tf = np.zeros((k1n, 2 * n2s, 2 * n2s))
    ti = np.zeros((k1n, 2 * n2s, 2 * n2s))
    for a in range(k1n):
        th = 2 * np.pi * np.outer(a + n1s * k2, n2) / n
        er, ei = np.cos(th), -np.sin(th)
        tf[a] = np.block([[er, -ei], [ei, er]])
        er, ei = np.cos(th.T), np.sin(th.T)
        ti[a] = np.block([[er, -ei], [ei, er]])
    eye = np.eye(FFT_SUB)
    return tuple(jnp.asarray(a, F32).astype(BF16) for a in (np.kron(m1, eye), np.kron(m1i, eye), tf, ti))


def _direct_dft(seq_len):
    n = 2 * seq_len
    th = 2 * np.pi * np.outer(np.arange(n), np.arange(seq_len)) / n
    fwd = np.concatenate([np.cos(th), -np.sin(th)], axis=0)
    inv = np.concatenate([np.cos(th.T), -np.sin(th.T)], axis=1) / n
    return jnp.asarray(fwd, F32).astype(BF16), jnp.asarray(inv, F32).astype(BF16)


def _filter_features(seq_len):
    t = np.linspace(0.0, 1.0, seq_len)[:, None]
    omega = (2.0 * math.pi / seq_len) * np.arange(seq_len, dtype=np.float64)
    bands = np.linspace(1e-4, FILTER_BANDS - 1, FILTER_BANDS)
    ang = omega[:, None] * bands[None, :]
    z = np.concatenate([t, np.cos(ang), -np.sin(ang)], axis=-1)
    z = np.pad(z, ((0, 0), (0, FILTER_WIDTH - FILTER_EMB)))
    rates = np.abs(np.linspace(math.log(DECAY_TARGET) / DECAY_FAST_PCT,
                               math.log(DECAY_TARGET) / DECAY_SLOW_PCT, D))[None]
    return jnp.asarray(z, F32), jnp.asarray(t, F32), jnp.asarray(rates, F32)


def _rms(x, g):
    return x * lax.rsqrt(jnp.mean(x * x, axis=-1, keepdims=True) + EPS) * g


def _norm_mod(h, g, shift, scale):
    return _rms(h, g) * (1.0 + scale) + shift


def _silu(x):
    return x * (1.0 / (1.0 + jnp.exp(-x)))


def _seg_masks(i):
    rows = i * TM + lax.broadcasted_iota(jnp.int32, (TM, 1), 0)
    seg = jnp.where(i < N_LAT_TILES, L, CTX)
    pos = rows & (seg - 1)
    return (pos != 0).astype(F32), (pos != seg - 1).astype(F32)


def _fill_xn(xn_ref, h_ref, hp_ref, hn_ref, g, shift, scale):
    xn_ref[0:HALO, :] = _norm_mod(hp_ref[...], g, shift, scale).astype(BF16)
    xn_ref[HALO:HALO + TM, :] = _norm_mod(h_ref[...], g, shift, scale).astype(BF16)
    xn_ref[HALO + TM:, :] = _norm_mod(hn_ref[...], g, shift, scale).astype(BF16)


def _conv3(u_ref, cw, cb, has_prev, has_next, n_rows):
    u = u_ref[...]
    total = n_rows + 2 * HALO
    u_prev = pltpu.roll(u, 1, 0)
    u_next = pltpu.roll(u, total - 1, 0)

    def piece(r0, r1, masked):
        prev = u_prev[HALO + r0:HALO + r1]
        nxt = u_next[HALO + r0:HALO + r1]
        if masked:
            prev = prev * has_prev[r0:r1]
            nxt = nxt * has_next[r0:r1]
        return cb + cw[0:1] * prev + cw[1:2] * u[HALO + r0:HALO + r1] + cw[2:3] * nxt

    pieces = []
    for s in range(0, n_rows, CTX):
        pieces += [piece(s, s + 8, True), piece(s + 8, s + CTX - 8, False), piece(s + CTX - 8, s + CTX, True)]
    return jnp.concatenate(pieces, axis=0)


def _mod_row(i):
    return jnp.where(i < N_LAT_TILES, i // (L // TM), B)


def _row_specs(in_rows):
    last = in_rows // HALO - 1
    per = TM // HALO
    return [
        pl.BlockSpec((TM, D), lambda i, *_: (i, 0)),
        pl.BlockSpec((HALO, D), lambda i, *_: (jnp.maximum(i * per - 1, 0), 0)),
        pl.BlockSpec((HALO, D), lambda i, *_: (jnp.minimum((i + 1) * per, last), 0)),
    ]


def _mods_kernel(s_ref, w_ref, b_ref, o_ref):
    s = _silu(s_ref[...])
    o_ref[...] = jnp.dot(s, w_ref[...], preferred_element_type=F32,
                         precision=lax.Precision.HIGHEST) + b_ref[...]


def _mods(cs, mod_w, mod_b):
    tn = 1536
    return pl.pallas_call(
        _mods_kernel,
        grid=(DEPTH, N_MOD * D // tn),
        in_specs=[
            pl.BlockSpec((8, D), lambda l, n: (0, 0)),
            pl.BlockSpec((None, D, tn), lambda l, n: (l, 0, n)),
            pl.BlockSpec((None, 1, tn), lambda l, n: (l, 0, n)),
        ],
        out_specs=pl.BlockSpec((None, 8, tn), lambda l, n: (l, 0, n)),
        out_shape=jax.ShapeDtypeStruct((DEPTH, 8, N_MOD * D), F32),
        compiler_params=_cparams(("parallel", "parallel"), 40),
        name="mods",
    )(cs, mod_w, mod_b.reshape(DEPTH, 1, N_MOD * D))


def _proj_attn_kernel(*refs, kind, assemble):
    if assemble:
        lat_ref, ctx_ref, refs, h_out_ref = refs[0], refs[1], refs[2:-1], refs[-1]
        is_lat = pl.program_id(0) < N_LAT_TILES
    else:
        h_ref, refs = refs[0], refs[1:]
    mod_ref, g_ref, w_ref, b_ref, qg_ref, kg_ref, cos_ref, sin_ref, o_ref = refs
    m = mod_ref[...]
    n_q = 8
    n_rope = 10
    q_scale = (A_DH if kind == "a" else C_DH) ** -0.5 * LOG2_E
    if kind == "c":
        lane = lax.broadcasted_iota(jnp.int32, (SUB_ROWS, 128), 1)
        first_half = (lane & (C_DH - 1)) < C_DH // 2
    for r0 in range(0, TM, SUB_ROWS):
        rows = slice(r0, r0 + SUB_ROWS)
        if assemble:
            h = jnp.where(is_lat, lat_ref[rows, :], ctx_ref[rows, :])
            h_out_ref[rows, :] = h
        else:
            h = h_ref[rows, :]
        xn = _norm_mod(h, g_ref[0:1, :], m[:, 0:D], m[:, D:2 * D]).astype(BF16)
        cos, sin = cos_ref[rows, :], sin_ref[rows, :]
        cos_q, sin_q = cos * q_scale, sin * q_scale
        for j in range(QKV_COLS // 128):
            if j % PROJ_GROUP == 0:
                wide = slice(j * 128, (j + PROJ_GROUP) * 128)
                y_wide = jnp.dot(xn, w_ref[:, wide], preferred_element_type=F32) + b_ref[:, wide]
            y = y_wide[:, (j % PROJ_GROUP) * 128:(j % PROJ_GROUP + 1) * 128]
            if j < n_rope:
                if kind == "a":
                    y = _rms(y, qg_ref[...] if j < n_q else kg_ref[...])
                    rot = pltpu.roll(y, 64, 1)
                else:
                    rot = jnp.where(first_half, pltpu.roll(y, 128 - C_DH // 2, 1), pltpu.roll(y, C_DH // 2, 1))
                y = y * cos_q + rot * sin_q if j < n_q else y * cos + rot * sin
            o_ref[rows, j * 128:(j + 1) * 128] = y.astype(BF16)


def _proj_attn(h, mods_l, g, w, b, q_gain, k_gain, cos_t, sin_t, kind):
    tpb = L // TM
    rope_idx = lambda i: (jnp.where(i < N_LAT_TILES, i % tpb, tpb), 0)
    assemble = isinstance(h, tuple)
    tile = pl.BlockSpec((TM, D), lambda i: (i, 0))
    if assemble:
        h_specs = [pl.BlockSpec((TM, D), lambda i: (jnp.minimum(i, N_LAT_TILES - 1), 0)),
                   pl.BlockSpec((TM, D), lambda i: (jnp.maximum(i - N_LAT_TILES, 0), 0))]
        h_args = list(h)
    else:
        h_specs, h_args = [tile], [h]
    qkv_spec = pl.BlockSpec((TM, QKV_COLS), lambda i: (i, 0))
    qkv_shape = jax.ShapeDtypeStruct((T, QKV_COLS), BF16)
    return pl.pallas_call(
        functools.partial(_proj_attn_kernel, kind=kind, assemble=assemble),
        grid=(N_TILES,),
        in_specs=h_specs + [
            pl.BlockSpec((None, 1, N_MOD * D), lambda i: (_mod_row(i), 0, 0)),
            pl.BlockSpec((4, D), lambda i: (0, 0)),
            pl.BlockSpec((D, QKV_COLS), lambda i: (0, 0)),
            pl.BlockSpec((1, QKV_COLS), lambda i: (0, 0)),
            pl.BlockSpec((1, 128), lambda i: (0, 0)),
            pl.BlockSpec((1, 128), lambda i: (0, 0)),
            pl.BlockSpec((TM, 128), rope_idx),
            pl.BlockSpec((TM, 128), rope_idx),
        ],
        out_specs=[qkv_spec, tile] if assemble else qkv_spec,
        out_shape=[qkv_shape, jax.ShapeDtypeStruct((T, D), F32)] if assemble else qkv_shape,
        compiler_params=_cparams(("parallel",), 40),
        name="proj_" + kind,
    )(*h_args, mods_l, g, w, b, q_gain, k_gain, cos_t, sin_t)


def _qk(q, k):
    return lax.dot_general(q, k, (((1,), (1,)), ((), ())), preferred_element_type=F32)


def _ones_ext(v):
    return jnp.concatenate([v, jnp.ones(v.shape, BF16)], axis=1)


def _attn_a_kernel(q_ref, kl_ref, vl_ref, kc_ref, vc_ref, o_ref, *, with_ctx):
    if with_ctx:
        is_ctx_step = pl.program_id(2) == L // TQA

        @pl.when(is_ctx_step)
        def _():
            _attn_a_body(q_ref, [[(kc_ref[...], vc_ref[...])]], o_ref)

        @pl.when(jnp.logical_not(is_ctx_step))
        def _():
            _attn_a_body(q_ref, _attn_a_latent_chunks(kl_ref, vl_ref, kc_ref, vc_ref), o_ref)
    else:
        _attn_a_body(q_ref, _attn_a_latent_chunks(kl_ref, vl_ref, kc_ref, vc_ref), o_ref)


def _attn_a_latent_chunks(kl_ref, vl_ref, kc_ref, vc_ref):
    chunks = [[(kc_ref[...], vc_ref[...]), (kl_ref[0:TK, :], vl_ref[0:TK, :])]]
    return chunks + [[(kl_ref[c * TK:(c + 1) * TK, :], vl_ref[c * TK:(c + 1) * TK, :])] for c in range(1, L // TK)]


def _attn_a_body(q_ref, chunks, o_ref):
    g = A_HEADS // A_KV
    m = [None] * g
    acc = [None] * g
    for ci, pieces in enumerate(chunks):
        v_exts = [_ones_ext(v) for _, v in pieces]
        scores = [jnp.concatenate([_qk(q_ref[:, i * A_DH:(i + 1) * A_DH], k) for k, _ in pieces], axis=1)
                  for i in range(g)]
        for i in range(g):
            s = scores[i]
            m_cur = jnp.broadcast_to(jnp.max(s, axis=-1, keepdims=True), (TQA, 128))
            m_new = m_cur if ci == 0 else jnp.maximum(m[i], m_cur)
            p = jnp.exp2(s - jnp.tile(m_new, (1, s.shape[1] // 128))).astype(BF16)
            pv, off = None, 0
            for (k, _), v_ext in zip(pieces, v_exts):
                part = jnp.dot(p[:, off:off + k.shape[0]], v_ext, preferred_element_type=F32)
                pv = part if pv is None else pv + part
                off += k.shape[0]
            if ci == 0:
                acc[i] = pv
            else:
                acc[i] = jnp.tile(jnp.exp2(m[i] - m_new), (1, 2)) * acc[i] + pv
            m[i] = m_new
    for i in range(g):
        o_ref[:, i * A_DH:(i + 1) * A_DH] = (acc[i][:, 0:A_DH] / acc[i][:, A_DH:]).astype(BF16)


def _attn_a(qkv, with_ctx):
    assert CTX == TQA
    nq = L // TQA
    gcols = (A_HEADS // A_KV) * A_DH // 128
    q_spec = pl.BlockSpec((TQA, gcols * 128),
                          lambda b, kv, j: (jnp.where(j < nq, b * nq + j, T_LAT // TQA + b), kv))
    return pl.pallas_call(
        functools.partial(_attn_a_kernel, with_ctx=with_ctx),
        grid=(B, A_KV, nq + with_ctx),
        in_specs=[
            q_spec,
            pl.BlockSpec((L, A_DH), lambda b, kv, j: (b, A_HEADS + kv)),
            pl.BlockSpec((L, A_DH), lambda b, kv, j: (b, A_HEADS + A_KV + kv)),
            pl.BlockSpec((CTX, A_DH), lambda b, kv, j: (T_LAT // CTX + b, A_HEADS + kv)),
            pl.BlockSpec((CTX, A_DH), lambda b, kv, j: (T_LAT // CTX + b, A_HEADS + A_KV + kv)),
        ],
        out_specs=q_spec,
        out_shape=jax.ShapeDtypeStruct((T if with_ctx else T_LAT, D), BF16),
        compiler_params=_cparams(("parallel", "parallel", "arbitrary"), 48),
        name="attn_a",
    )(qkv, qkv, qkv, qkv, qkv)


def _attn_c_kernel(q_ref, kl_ref, vl_ref, kx_ref, vx_ref, sink_ref, o_ref):
    j = pl.program_id(2)
    g = C_HEADS // C_KV
    lane = lax.broadcasted_iota(jnp.int32, (TQ, 128), 1)
    lo = lane < C_DH
    zero = jnp.zeros((TQ, 128), BF16)
    is_lat = j < NQ_LAT
    q0 = j * TQ
    win = TQ + 2 * WINDOW
    s0 = pl.multiple_of(jnp.clip(q0 - WINDOW, 0, L - win), 128)
    k_all = jnp.concatenate([kx_ref[...], kl_ref[pl.ds(s0, win), :]], axis=0)
    v_ext = _ones_ext(jnp.concatenate([vx_ref[...], vl_ref[pl.ds(s0, win), :]], axis=0))
    rel = (lax.broadcasted_iota(jnp.int32, (TQ, win), 1) - lax.broadcasted_iota(jnp.int32, (TQ, win), 0)
           + jnp.where(is_lat, s0 - q0, 4 * win))
    bias = jnp.concatenate([jnp.zeros((TQ, CTX), F32), jnp.where(jnp.abs(rel) <= WINDOW, 0.0, NEG_INF)], axis=1)
    heads = []
    for p in range(g // 2):
        qb = q_ref[:, p * 128:(p + 1) * 128]
        heads += [jnp.where(lo, qb, zero), jnp.where(lo, zero, qb)]
    scores = [_qk(qh, k_all) + bias for qh in heads]
    outs = []
    for h in range(g):
        s = scores[h]
        sink = jnp.tile(sink_ref[h * 8:(h + 1) * 8, :], (TQ // 8, 1))
        m_part = sink
        for c in range(s.shape[1] // 128):
            m_part = jnp.maximum(m_part, s[:, c * 128:(c + 1) * 128])
        m = jnp.broadcast_to(jnp.max(m_part, axis=-1, keepdims=True), (TQ, 128))
        p = jnp.exp2(s - jnp.tile(m, (1, s.shape[1] // 128))).astype(BF16)
        pv = jnp.dot(p, v_ext, preferred_element_type=F32)
        outs.append(pv[:, 0:128] / (pv[:, 128:] + jnp.exp2(sink - m)))
    for p in range(g // 2):
        o_ref[:, p * 128:(p + 1) * 128] = jnp.where(lo, outs[2 * p], outs[2 * p + 1]).astype(BF16)


def _attn_c(qkv, sink_rows):
    nq = NQ_LAT + NQ_CTX
    qcols = C_HEADS * C_DH // C_KV
    kcol = C_HEADS * C_DH // 128
    vcol = kcol + C_KV
    n_lat_blocks = T_LAT // TQ

    def q_idx(b, kv, j):
        return (jnp.where(j < NQ_LAT, b * NQ_LAT + j, n_lat_blocks + b * NQ_CTX + (j - NQ_LAT)), kv)

    lat = lambda col: pl.BlockSpec((L, 128), lambda b, kv, j: (b, col + kv))
    ctx = lambda col: pl.BlockSpec((CTX, 128), lambda b, kv, j: (T_LAT // CTX + b, col + kv))
    return pl.pallas_call(
        _attn_c_kernel,
        grid=(B, C_KV, nq),
        in_specs=[
            pl.BlockSpec((TQ, qcols), q_idx),
            lat(kcol), lat(vcol), ctx(kcol), ctx(vcol),
            pl.BlockSpec((None, (C_HEADS // C_KV) * 8, 128), lambda b, kv, j: (kv, 0, 0)),
        ],
        out_specs=pl.BlockSpec((TQ, qcols), q_idx),
        out_shape=jax.ShapeDtypeStruct((T, D), BF16),
        compiler_params=_cparams(("parallel", "parallel", "arbitrary"), 48),
        name="attn_c",
    )(qkv, qkv, qkv, qkv, qkv, sink_rows)


def _proj_b_kernel(h_ref, hp_ref, hn_ref, mod_ref, g_ref, w_ref, b_ref, cw_ref, cb_ref,
                   x0_ref, z_ref, xn_ref, u_ref):
    i = pl.program_id(0)
    m = mod_ref[...]
    _fill_xn(xn_ref, h_ref, hp_ref, hn_ref, g_ref[0:1, :], m[:, 0:D], m[:, D:2 * D])
    has_prev, has_next = _seg_masks(i)

    def matmul(k):
        u_ref[k % 2] = (jnp.dot(xn_ref[...], w_ref[:, k * D:(k + 1) * D], preferred_element_type=F32)
                        + b_ref[:, k * D:(k + 1) * D])

    matmul(0)
    x1 = None
    for k in range(3):
        if k + 1 < 3:
            matmul(k + 1)
        cols = slice(k * D, (k + 1) * D)
        conv = _conv3(u_ref.at[k % 2], cw_ref[:, cols], cb_ref[:, cols], has_prev, has_next, TM)
        if k == 0:
            x0_ref[...] = conv.astype(BF16)
        elif k == 1:
            x1 = conv
        else:
            z_ref[...] = (conv * x1).astype(BF16)


def _proj_b(h, mods_l, g, w, b, cw, cb):
    full = lambda shape: pl.BlockSpec(shape, lambda i: (0,) * len(shape))
    return pl.pallas_call(
        _proj_b_kernel,
        grid=(N_TILES,),
        in_specs=_row_specs(T) + [
            pl.BlockSpec((None, 1, N_MOD * D), lambda i: (_mod_row(i), 0, 0)),
            full((4, D)), full((D, 3 * D)), full((1, 3 * D)), full((3, 3 * D)), full((1, 3 * D)),
        ],
        out_specs=[pl.BlockSpec((TM, D), lambda i: (i, 0))] * 2,
        out_shape=[jax.ShapeDtypeStruct((T, D), BF16)] * 2,
        scratch_shapes=[pltpu.VMEM((TM + 2 * HALO, D), BF16), pltpu.VMEM((2, TM + 2 * HALO, D), F32)],
        compiler_params=_cparams(("parallel",), 48),
        name="proj_b",
    )(h, h, h, mods_l, g, w, b, cw, cb)


def _filter_kernel(zf_ref, t_ref, w1_ref, b1_ref, w2_ref, b2_ref, w3_ref, b3_ref, w4_ref, rates_ref, o_ref):
    i = pl.program_id(0)
    tm = zf_ref.shape[0]
    dot = functools.partial(jnp.dot, preferred_element_type=F32, precision=lax.Precision.HIGHEST)
    f = jnp.sin(FILTER_SIN_FREQ * (dot(zf_ref[...], w1_ref[...]) + b1_ref[...]))
    f = jnp.sin(FILTER_SIN_FREQ * (dot(f, w2_ref[...]) + b2_ref[...]))
    f = jnp.sin(FILTER_SIN_FREQ * (dot(f, w3_ref[...]) + b3_ref[...]))
    f = jnp.dot(f.astype(BF16), w4_ref[...].astype(BF16), preferred_element_type=F32)
    decay = jnp.exp(-t_ref[...] * rates_ref[...])
    row = i * tm + lax.broadcasted_iota(jnp.int32, (tm, 1), 0)
    o_ref[:, 0:D] = (f[:, 0:D] * decay).astype(BF16)
    o_ref[:, D:2 * D] = (f[:, D:2 * D] * decay * (row != 0).astype(F32)).astype(BF16)


def _hyena_filter(seq_len, w1, b1, w2, b2, w3, b3, w4):
    zf, t, rates = _filter_features(seq_len)
    tm = min(seq_len, 512)
    w1p = jnp.pad(w1, ((0, FILTER_WIDTH - FILTER_EMB), (0, 0)))
    full = lambda shape: pl.BlockSpec(shape, lambda i: (0,) * len(shape))
    fw = FILTER_WIDTH
    return pl.pallas_call(
        _filter_kernel,
        grid=(seq_len // tm,),
        in_specs=[
            pl.BlockSpec((tm, fw), lambda i: (i, 0)),
            pl.BlockSpec((tm, 1), lambda i: (i, 0)),
            full((fw, fw)), full((1, fw)), full((fw, fw)), full((1, fw)), full((fw, fw)), full((1, fw)),
            full((fw, 2 * D)), full((1, D)),
        ],
        out_specs=pl.BlockSpec((tm, 2 * D), lambda i: (i, 0)),
        out_shape=jax.ShapeDtypeStruct((seq_len, 2 * D), BF16),
        compiler_params=_cparams(("parallel",), 40),
        name="hyena_filter",
    )(zf, t, w1p, b1.reshape(1, fw), w2, b2.reshape(1, fw), w3, b3.reshape(1, fw), w4, rates)


def _dft_forward(x_ref, xs_ref, a_ref, m1_ref, tf_ref, emit):
    ct = x_ref.shape[1]
    xs_ref[...] = x_ref[...].astype(F32).reshape(FFT_H1, FFT_N2, ct)

    def stage1(g, carry):
        rows = pl.ds(pl.multiple_of(g * FFT_SUB, FFT_SUB), FFT_SUB)
        x = xs_ref[:, rows, :].reshape(FFT_H1 * FFT_SUB, ct).astype(BF16)
        a = jnp.dot(m1_ref[...], x, preferred_element_type=F32)
        a_ref[:, rows, :] = a.reshape(FFT_R, FFT_SUB, ct)
        return carry

    lax.fori_loop(0, FFT_N2 // FFT_SUB, stage1, 0, unroll=FFT_UNROLL1)

    def stage2(k1, carry):
        d = a_ref[pl.ds(2 * k1, 2)].reshape(2 * FFT_N2, ct).astype(BF16)
        emit(k1, jnp.dot(tf_ref[k1], d, preferred_element_type=F32))
        return carry

    lax.fori_loop(0, FFT_K1, stage2, 0, unroll=FFT_UNROLL)


def _filter_spectrum_kernel(hf_ref, hb_ref, m1_ref, tf_ref, o_ref, xs_ref, a_ref, f_ref):
    half = FFT_N2

    def keep(k1, x):
        f_ref[k1] = x

    _dft_forward(hf_ref, xs_ref, a_ref, m1_ref, tf_ref, keep)

    def combine(k1, x):
        f = f_ref[k1]
        o_ref[k1, 0:half, :] = (f[0:half] + x[0:half]).astype(BF16)
        o_ref[k1, half:, :] = (f[half:] - x[half:]).astype(BF16)

    _dft_forward(hb_ref, xs_ref, a_ref, m1_ref, tf_ref, combine)


def _long_conv_kernel(z_ref, hf_ref, m1_ref, tf_ref, ti_ref, m1i_ref, _, y_ref, xs_ref, a_ref, g_ref):
    half = FFT_N2
    ct = z_ref.shape[1]
    g_ref[2 * FFT_K1:] = jnp.zeros((FFT_R - 2 * FFT_K1, half, ct), F32)

    def product_and_invert(k1, x):
        hr = hf_ref[k1, 0:half, :].astype(F32)
        hi = hf_ref[k1, half:, :].astype(F32)
        xr, xi = x[0:half], x[half:]
        p = jnp.concatenate([xr * hr - xi * hi, xr * hi + xi * hr], axis=0).astype(BF16)
        g_ref[pl.ds(2 * k1, 2)] = jnp.dot(ti_ref[k1], p, preferred_element_type=F32).reshape(2, half, ct)

    _dft_forward(z_ref, xs_ref, a_ref, m1_ref, tf_ref, product_and_invert)

    def inverse_stage1(g, carry):
        rows = pl.ds(pl.multiple_of(g * FFT_SUB, FFT_SUB), FFT_SUB)
        x = g_ref[:, rows, :].reshape(FFT_R * FFT_SUB, ct).astype(BF16)
        y = jnp.dot(m1i_ref[...], x, preferred_element_type=F32)
        xs_ref[:, rows, :] = y.reshape(FFT_H1, FFT_SUB, ct)
        return carry

    lax.fori_loop(0, FFT_N2 // FFT_SUB, inverse_stage1, 0, unroll=FFT_UNROLL1)
    y_ref[...] = xs_ref[...].reshape(L, ct).astype(y_ref.dtype)


def _long_conv_latent(z, filt, spare):
    m1, m1i, tf, ti = _two_stage_dft()
    ct = FFT_CT
    n_ct = D // ct
    const = lambda shape: pl.BlockSpec(shape, lambda *_: (0,) * len(shape))
    scratch = [pltpu.VMEM((FFT_H1, FFT_N2, ct), F32), pltpu.VMEM((FFT_R, FFT_N2, ct), F32)]
    hf = pl.pallas_call(
        _filter_spectrum_kernel,
        grid=(n_ct,),
        in_specs=[
            pl.BlockSpec((L, ct), lambda c: (0, c)),
            pl.BlockSpec((L, ct), lambda c: (0, n_ct + c)),
            const(m1.shape), const(tf.shape),
        ],
        out_specs=pl.BlockSpec((FFT_K1, 2 * FFT_N2, ct), lambda c: (0, 0, c)),
        out_shape=jax.ShapeDtypeStruct((FFT_K1, 2 * FFT_N2, D), BF16),
        scratch_shapes=scratch + [pltpu.VMEM((FFT_K1, 2 * FFT_N2, ct), F32)],
        compiler_params=_cparams(("parallel",), 48),
        name="filter_spectrum",
    )(filt, filt, m1, tf)
    return pl.pallas_call(
        _long_conv_kernel,
        grid=(n_ct, B),
        in_specs=[
            pl.BlockSpec((L, ct), lambda c, b: (b, c)),
            pl.BlockSpec((FFT_K1, 2 * FFT_N2, ct), lambda c, b: (0, 0, c)),
            const(m1.shape), const(tf.shape), const(ti.shape), const(m1i.shape),
            pl.BlockSpec(memory_space=pl.ANY),
        ],
        out_specs=pl.BlockSpec((L, ct), lambda c, b: (b, c)),
        out_shape=jax.ShapeDtypeStruct((T, D), BF16),
        input_output_aliases={6: 0},
        scratch_shapes=scratch + [pltpu.VMEM((FFT_R, FFT_N2, ct), F32)],
        compiler_params=_cparams(("parallel", "arbitrary"), 56),
        name="long_conv",
    )(z, hf, m1, tf, ti, m1i, spare)


def _ctx_conv_kernel(z_ref, filt_ref, fwd_ref, inv_ref, _, y_ref):
    half = fwd_ref.shape[0] // 2
    xf = jnp.dot(fwd_ref[...], filt_ref[...], preferred_element_type=F32)
    hr = xf[0:half, 0:D] + xf[0:half, D:2 * D]
    hi = xf[half:, 0:D] - xf[half:, D:2 * D]
    x = jnp.dot(fwd_ref[...], z_ref[...], preferred_element_type=F32)
    xr, xi = x[0:half], x[half:]
    p = jnp.concatenate([xr * hr - xi * hi, xr * hi + xi * hr], axis=0).astype(BF16)
    y_ref[...] = jnp.dot(inv_ref[...], p, preferred_element_type=F32).astype(y_ref.dtype)


def _hyena_long_conv(z, filt_lat, filt_ctx, spare):
    y = _long_conv_latent(z, filt_lat, spare)
    fwd, inv = _direct_dft(CTX)
    full = lambda shape: pl.BlockSpec(shape, lambda b: (0,) * len(shape))
    ctx_rows = pl.BlockSpec((CTX, D), lambda b: (T_LAT // CTX + b, 0))
    return pl.pallas_call(
        _ctx_conv_kernel,
        grid=(B,),
        in_specs=[ctx_rows, full(filt_ctx.shape), full(fwd.shape), full(inv.shape),
                  pl.BlockSpec(memory_space=pl.ANY)],
        out_specs=ctx_rows,
        out_shape=jax.ShapeDtypeStruct((T, D), BF16),
        input_output_aliases={4: 0},
        compiler_params=_cparams(("parallel",), 48),
        name="ctx_conv",
    )(z, filt_ctx, fwd, inv, y)


def _layer_tail_kernel(*refs, hyena):
    h3, y3, refs = refs[0:3], refs[3:6], refs[6:]
    if hyena:
        z3, x03, skip_ref, refs = refs[0:3], refs[3:6], refs[6], refs[7:]
    mod_ref, g_ref, wo_ref, bo_ref, wup_ref, cw_ref, cb_ref, wd_ref, o_ref, xn_ref, gate_ref = refs
    i = pl.program_id(0)
    m = mod_ref[...]
    half = TM // 2
    ext_half = half + HALO

    def ext_rows(trio, s):
        prev_ref, main_ref, next_ref = trio[1], trio[0], trio[2]
        if s == 0:
            return jnp.concatenate([prev_ref[...], main_ref[0:half, :]], axis=0)
        return jnp.concatenate([main_ref[half:, :], next_ref[...]], axis=0)

    for s in range(2):
        mix = ext_rows(y3, s)
        if hyena:
            mix = ((mix.astype(F32) + ext_rows(z3, s).astype(F32) * skip_ref[...])
                   * ext_rows(x03, s).astype(F32)).astype(BF16)
        r = jnp.dot(mix, wo_ref[...], preferred_element_type=F32) + bo_ref[...]
        h_mid = ext_rows(h3, s) + m[:, 2 * D:3 * D] * _rms(r, g_ref[1:2, :])
        xn_ref[s * ext_half:(s + 1) * ext_half, :] = _norm_mod(
            h_mid, g_ref[2:3, :], m[:, 3 * D:4 * D], m[:, 4 * D:5 * D]).astype(BF16)
        o_ref[s * half:(s + 1) * half, :] = h_mid[HALO:, :] if s == 0 else h_mid[0:half, :]

    has_prev, has_next = _seg_masks(i)
    ext = SUB_ROWS + 2 * HALO
    n_sb = TM // SUB_ROWS
    units = [(c, sb) for c in range(N_FC) for sb in range(n_sb)]

    def up_matmuls(k):
        c, sb = units[k]
        r0 = sb * SUB_ROWS
        gate_ref[k % 2] = jnp.dot(xn_ref[r0:r0 + ext, :], wup_ref[:, c * FC:(c + 1) * FC],
                                  preferred_element_type=F32)
        return jnp.dot(xn_ref[HALO + r0:HALO + r0 + SUB_ROWS, :],
                       wup_ref[:, D_FF + c * FC:D_FF + (c + 1) * FC], preferred_element_type=F32)

    acc = [None] * n_sb
    up_next = up_matmuls(0)
    for k, (c, sb) in enumerate(units):
        up = up_next
        if k + 1 < len(units):
            up_next = up_matmuls(k + 1)
        cols = slice(c * FC, (c + 1) * FC)
        rows = slice(sb * SUB_ROWS, (sb + 1) * SUB_ROWS)
        conv = _conv3(gate_ref.at[k % 2], cw_ref[:, cols], cb_ref[:, cols], has_prev[rows], has_next[rows], SUB_ROWS)
        part = jnp.dot((_silu(conv) * up).astype(BF16), wd_ref[cols, :], preferred_element_type=F32)
        acc[sb] = part if c == 0 else acc[sb] + part
    for sb in range(n_sb):
        rows = slice(sb * SUB_ROWS, (sb + 1) * SUB_ROWS)
        o_ref[rows, :] = o_ref[rows, :] + m[:, 5 * D:6 * D] * _rms(acc[sb], g_ref[3:4, :])


def _layer_tail(h, y, mods_l, g, w_out, b_out, layer, w_up, cw, cb, w_down, n_tiles, hyena_args=None):
    resident = lambda shape: pl.BlockSpec(shape, lambda i: (0,) * len(shape), pipeline_mode=pl.Buffered(1))
    stacked = lambda shape: pl.BlockSpec((None,) + shape, lambda i: (layer,) + (0,) * len(shape),
                                         pipeline_mode=pl.Buffered(1))
    hyena = hyena_args is not None
    in_specs = _row_specs(h.shape[0]) + _row_specs(y.shape[0])
    args = [h, h, h, y, y, y]
    if hyena:
        z, x0, skip = hyena_args
        in_specs += _row_specs(z.shape[0]) + _row_specs(x0.shape[0]) + [resident((1, D))]
        args += [z, z, z, x0, x0, x0, skip]
    in_specs += [
        pl.BlockSpec((None, 1, N_MOD * D), lambda i: (_mod_row(i), 0, 0)),
        pl.BlockSpec((4, D), lambda i: (0, 0)),
        resident((D, D)), resident((1, D)),
        stacked((D, 2 * D_FF)), stacked((3, D_FF)), stacked((1, D_FF)), stacked((D_FF, D)),
    ]
    args += [mods_l, g, w_out, b_out, w_up, cw, cb.reshape(DEPTH, 1, D_FF), w_down]
    return pl.pallas_call(
        functools.partial(_layer_tail_kernel, hyena=hyena),
        grid=(n_tiles,),
        in_specs=in_specs,
        out_specs=pl.BlockSpec((TM, D), lambda i: (i, 0)),
        out_shape=jax.ShapeDtypeStruct((n_tiles * TM, D), F32),
        scratch_shapes=[
            pltpu.VMEM((TM + 2 * HALO, D), BF16),
            pltpu.VMEM((2, SUB_ROWS + 2 * HALO, FC), F32),
        ],
        compiler_params=_cparams(("parallel",), 56),
        name="layer_tail",
    )(*args)


def _dup_heads(w):
    k0, k1 = w[..., 0:C_DH], w[..., C_DH:2 * C_DH]
    return jnp.concatenate([k0, k0, k1, k1], axis=-1)


def kernel(x, c, ctx, c_ctx, mod_w, mod_b, norm_g, ffn_w_up, ffn_conv_w, ffn_conv_b, ffn_w_down, a_w_qkv, a_q_gain, a_k_gain, a_w_out, b_w_in, b_b_in, b_conv_w, b_conv_b, b_f_w1, b_f_b1, b_f_w2, b_f_b2, b_f_w3, b_f_b3, b_f_w4, b_skip, b_w_out, b_b_out, c_w_qkv, c_b_qkv, c_sink, c_w_out, c_b_out):
    h = (x.reshape(T_LAT, D), ctx.reshape(T_CTX, D))
    cs = jnp.concatenate([c, c_ctx[None], jnp.zeros((8 - B - 1, D), F32)], axis=0)
    mods = _mods(cs, mod_w, mod_b).reshape(DEPTH, 8, 1, N_MOD * D)
    cos_a, sin_a = _rope_tables(A_DH)
    cos_c, sin_c = _rope_tables(C_DH)
    zeros_d = jnp.zeros((1, D), F32)
    w_up_bf16, w_down_bf16 = ffn_w_up.astype(BF16), ffn_w_down.astype(BF16)
    y = None

    for i in range(DEPTH):
        last = i == DEPTH - 1
        kind, j = i % 3, i // 3
        g = norm_g[i]
        mods_l = mods[i]
        n_tiles = N_LAT_TILES if last else N_TILES
        if kind == 0:
            qkv = _proj_attn(h, mods_l, g, a_w_qkv[j].astype(BF16), jnp.zeros((1, QKV_COLS), F32),
                             a_q_gain[j].reshape(1, A_DH), a_k_gain[j].reshape(1, A_DH), cos_a, sin_a, "a")
            if isinstance(h, tuple):
                qkv, h = qkv
            y = _attn_a(qkv, with_ctx=not last)
            w_out, b_out, hyena_args = a_w_out[j], zeros_d, None
        elif kind == 1:
            x0, z = _proj_b(h, mods_l, g, b_w_in[j].astype(BF16), b_b_in[j].reshape(1, 3 * D),
                            b_conv_w[j], b_conv_b[j].reshape(1, 3 * D))
            fw = (b_f_w1[j], b_f_b1[j], b_f_w2[j], b_f_b2[j], b_f_w3[j], b_f_b3[j], b_f_w4[j])
            spare = y if y is not None and y.shape == (T, D) else jnp.zeros((T, D), BF16)
            y = _hyena_long_conv(z, _hyena_filter(L, *fw), _hyena_filter(CTX, *fw), spare)
            w_out, b_out = b_w_out[j], b_b_out[j].reshape(1, D)
            hyena_args = (z, x0, b_skip[j].reshape(1, D))
        else:
            qc = C_HEADS * C_DH
            kc = qc + C_KV * C_DH
            w, bias = c_w_qkv[j], c_b_qkv[j].reshape(1, -1)
            w = jnp.concatenate([w[:, :qc], _dup_heads(w[:, qc:kc]), _dup_heads(w[:, kc:])], axis=1)
            bias = jnp.concatenate([bias[:, :qc], _dup_heads(bias[:, qc:kc]), _dup_heads(bias[:, kc:])], axis=1)
            ones = jnp.ones((1, 128), F32)
            qkv = _proj_attn(h, mods_l, g, w.astype(BF16), bias, ones, ones, cos_c, sin_c, "c")
            sink_rows = jnp.broadcast_to((c_sink[j].astype(F32) * LOG2_E)[:, None, None],
                                         (C_HEADS, 8, 128)).reshape(C_KV, (C_HEADS // C_KV) * 8, 128)
            y = _attn_c(qkv, sink_rows)
            w_out, b_out, hyena_args = c_w_out[j], c_b_out[j].reshape(1, D), None
        h = _layer_tail(h, y, mods_l, g, w_out.astype(BF16), b_out, i, w_up_bf16, ffn_conv_w, ffn_conv_b,
                        w_down_bf16, n_tiles, hyena_args)
    return h.reshape(B, L, D)
```

```python
import functools
import math

import numpy as np
import jax
import jax.numpy as jnp
from jax import lax
from jax.experimental import pallas as pl
from jax.experimental.pallas import tpu as pltpu

F32 = jnp.float32
BF16 = jnp.bfloat16

D = 1024
B = 4
L = 4096
CTX = 256
DEPTH = 4
N_MOD = 6
EPS = 1e-6
NEG_INF = -1e30
LOG2_E = math.log2(math.e)
GRID_W = 64
ROPE_THETA = 10000.0

A_HEADS, A_KV, A_DH = 8, 2, 128
C_HEADS, C_KV, C_DH = 16, 2, 64
QKV_COLS = 1536
PROJ_GROUP = 4
SUB_ROWS = 256
D_FF = 2816
FILTER_BANDS = 16
FILTER_EMB = 1 + 2 * FILTER_BANDS
FILTER_WIDTH = 64
FILTER_SIN_FREQ = 1.0
DECAY_TARGET = 1e-2
DECAY_FAST_PCT = 0.3
DECAY_SLOW_PCT = 1.5

T_LAT = B * L
T_CTX = B * CTX
T = T_LAT + T_CTX

TM = 512
HALO = 16
N_LAT_TILES = T_LAT // TM
N_TILES = T // TM
MXU_W = 256
FC_CHUNKS = ((0, 6 * MXU_W), (6 * MXU_W, 5 * MXU_W))
FC = max(w for _, w in FC_CHUNKS)
WINDOW = 128
TQ = 256
TQA = 256
TK = 4096
NQ_LAT = L // TQ
NQ_CTX = CTX // TQ

FFT_N = 2 * L
FFT_N1 = 64
FFT_N2 = FFT_N // FFT_N1
FFT_H1 = FFT_N1 // 2
FFT_K1 = FFT_H1 + 1
FFT_R = 72
FFT_CT = 256
FFT_SUB = 8
FFT_UNROLL = 11
FFT_UNROLL1 = 4

VMEM_MB = 1024 * 1024


def _cparams(sem, vmem_mb):
    return pltpu.CompilerParams(dimension_semantics=sem, vmem_limit_bytes=vmem_mb * VMEM_MB)


def _rope_tables(head_dim):
    rows = L // GRID_W
    row = np.repeat(np.arange(rows), GRID_W).astype(np.float64)
    col = np.tile(np.arange(GRID_W), rows).astype(np.float64)
    axis_dim = head_dim // 2
    inv = np.power(ROPE_THETA, -np.arange(0, axis_dim, 2, dtype=np.float64) / axis_dim)
    ang = np.concatenate([row[:, None] * inv[None], col[:, None] * inv[None]], axis=-1)
    cos, sin = np.cos(ang), np.sin(ang)
    reps = 128 // head_dim
    cos_t = np.tile(np.concatenate([cos, cos], axis=-1), (1, reps))
    sin_t = np.tile(np.concatenate([-sin, sin], axis=-1), (1, reps))
    cos_t = np.concatenate([cos_t, np.ones((TM, 128))], axis=0)
    sin_t = np.concatenate([sin_t, np.zeros((TM, 128))], axis=0)
    return jnp.asarray(cos_t, F32), jnp.asarray(sin_t, F32)


def _two_stage_dft():
    n1s, n2s, h1, k1n, n = FFT_N1, FFT_N2, FFT_H1, FFT_K1, FFT_N
    n1 = np.arange(h1)
    k1 = np.arange(k1n)
    ang = 2 * np.pi * np.outer(k1, n1) / n1s
    m1 = np.zeros((FFT_R, h1))
    m1[0:2 * k1n:2] = np.cos(ang)
    m1[1:2 * k1n:2] = -np.sin(ang)
    c = np.where((k1 == 0) | (k1 == h1), 1.0, 2.0)
    m1i = np.zeros((h1, FFT_R))
    m1i[:, 0:2 * k1n:2] = (c[None] / n) * np.cos(ang.T)
    m1i[:, 1:2 * k1n:2] = -(c[None] / n) * np.sin(ang.T)
    n2 = np.arange(n2s)
    k2 = np.arange(n2s)
    tf = np.zeros((k1n, 2 * n2s, 2 * n2s))
    ti = np.zeros((k1n, 2 * n2s, 2 * n2s))
    for a in range(k1n):
        th = 2 * np.pi * np.outer(a + n1s * k2, n2) / n
        er, ei = np.cos(th), -np.sin(th)
        tf[a] = np.block([[er, -ei], [ei, er]])
        er, ei = np.cos(th.T), np.sin(th.T)
        ti[a] = np.block([[er, -ei], [ei, er]])
    eye = np.eye(FFT_SUB)
    return tuple(jnp.asarray(a, F32).astype(BF16) for a in (np.kron(m1, eye), np.kron(m1i, eye), tf, ti))


def _direct_dft(seq_len):
    n = 2 * seq_len
    th = 2 * np.pi * np.outer(np.arange(n), np.arange(seq_len)) / n
    fwd = np.concatenate([np.cos(th), -np.sin(th)], axis=0)
    inv = np.concatenate([np.cos(th.T), -np.sin(th.T)], axis=1) / n
    return jnp.asarray(fwd, F32).astype(BF16), jnp.asarray(inv, F32).astype(BF16)


def _filter_features(seq_len):
    t = np.linspace(0.0, 1.0, seq_len)[:, None]
    omega = (2.0 * math.pi / seq_len) * np.arange(seq_len, dtype=np.float64)
    bands = np.linspace(1e-4, FILTER_BANDS - 1, FILTER_BANDS)
    ang = omega[:, None] * bands[None, :]
    z = np.concatenate([t, np.cos(ang), -np.sin(ang)], axis=-1)
    z = np.pad(z, ((0, 0), (0, FILTER_WIDTH - FILTER_EMB)))
    rates = np.abs(np.linspace(math.log(DECAY_TARGET) / DECAY_FAST_PCT,
                               math.log(DECAY_TARGET) / DECAY_SLOW_PCT, D))[None]
    return jnp.asarray(z, F32), jnp.asarray(t, F32), jnp.asarray(rates, F32)


def _rms(x, g):
    return x * lax.rsqrt(jnp.mean(x * x, axis=-1, keepdims=True) + EPS) * g


def _norm_mod(h, g, shift, scale):
    return _rms(h, g) * (1.0 + scale) + shift


def _silu(x):
    return x * (1.0 / (1.0 + jnp.exp(-x)))


def _seg_masks(i):
    rows = i * TM + lax.broadcasted_iota(jnp.int32, (TM, 1), 0)
    seg = jnp.where(i < N_LAT_TILES, L, CTX)
    pos = rows & (seg - 1)
    return (pos != 0).astype(F32), (pos != seg - 1).astype(F32)


def _fill_xn(xn_ref, h_ref, hp_ref, hn_ref, g, shift, scale):
    xn_ref[0:HALO, :] = _norm_mod(hp_ref[...], g, shift, scale).astype(BF16)
    xn_ref[HALO:HALO + TM, :] = _norm_mod(h_ref[...], g, shift, scale).astype(BF16)
    xn_ref[HALO + TM:, :] = _norm_mod(hn_ref[...], g, shift, scale).astype(BF16)


def _conv3(u_ref, cw, cb, has_prev, has_next, n_rows):
    u = u_ref[...]
    total = n_rows + 2 * HALO
    u_prev = pltpu.roll(u, 1, 0)
    u_next = pltpu.roll(u, total - 1, 0)

    def piece(r0, r1, masked):
        prev = u_prev[HALO + r0:HALO + r1]
        nxt = u_next[HALO + r0:HALO + r1]
        if masked:
            prev = prev * has_prev[r0:r1]
            nxt = nxt * has_next[r0:r1]
        return cb + cw[0:1] * prev + cw[1:2] * u[HALO + r0:HALO + r1] + cw[2:3] * nxt

    pieces = []
    for s in range(0, n_rows, CTX):
        pieces += [piece(s, s + 8, True), piece(s + 8, s + CTX - 8, False), piece(s + CTX - 8, s + CTX, True)]
    return jnp.concatenate(pieces, axis=0)


def _mod_row(i):
    return jnp.where(i < N_LAT_TILES, i // (L // TM), B)


def _row_specs(in_rows):
    last = in_rows // HALO - 1
    per = TM // HALO
    return [
        pl.BlockSpec((TM, D), lambda i, *_: (i, 0)),
        pl.BlockSpec((HALO, D), lambda i, *_: (jnp.maximum(i * per - 1, 0), 0)),
        pl.BlockSpec((HALO, D), lambda i, *_: (jnp.minimum((i + 1) * per, last), 0)),
    ]


def _mods_kernel(s_ref, w_ref, b_ref, o_ref):
    s = _silu(s_ref[...])
    o_ref[...] = jnp.dot(s, w_ref[...], preferred_element_type=F32,
                         precision=lax.Precision.HIGHEST) + b_ref[...]


def _mods(cs, mod_w, mod_b):
    tn = 1536
    return pl.pallas_call(
        _mods_kernel,
        grid=(DEPTH, N_MOD * D // tn),
        in_specs=[
            pl.BlockSpec((8, D), lambda l, n: (0, 0)),
            pl.BlockSpec((None, D, tn), lambda l, n: (l, 0, n)),
            pl.BlockSpec((None, 1, tn), lambda l, n: (l, 0, n)),
        ],
        out_specs=pl.BlockSpec((None, 8, tn), lambda l, n: (l, 0, n)),
        out_shape=jax.ShapeDtypeStruct((DEPTH, 8, N_MOD * D), F32),
        compiler_params=_cparams(("parallel", "parallel"), 40),
        name="mods",
    )(cs, mod_w, mod_b.reshape(DEPTH, 1, N_MOD * D))


def _proj_attn_kernel(*refs, kind, assemble):
    if assemble:
        lat_ref, ctx_ref, refs, h_out_ref = refs[0], refs[1], refs[2:-1], refs[-1]
        is_lat = pl.program_id(0) < N_LAT_TILES
    else:
        h_ref, refs = refs[0], refs[1:]
    mod_ref, g_ref, w_ref, b_ref, qg_ref, kg_ref, cos_ref, sin_ref, o_ref = refs
    m = mod_ref[...]
    n_q = 8
    n_rope = 10
    q_scale = (A_DH if kind == "a" else C_DH) ** -0.5 * LOG2_E
    if kind == "c":
        lane = lax.broadcasted_iota(jnp.int32, (SUB_ROWS, 128), 1)
        first_half = (lane & (C_DH - 1)) < C_DH // 2
    for r0 in range(0, TM, SUB_ROWS):
        rows = slice(r0, r0 + SUB_ROWS)
        if assemble:
            h = jnp.where(is_lat, lat_ref[rows, :], ctx_ref[rows, :])
            h_out_ref[rows, :] = h
        else:
            h = h_ref[rows, :]
        xn = _norm_mod(h, g_ref[0:1, :], m[:, 0:D], m[:, D:2 * D]).astype(BF16)
        cos, sin = cos_ref[rows, :], sin_ref[rows, :]
        cos_q, sin_q = cos * q_scale, sin * q_scale
        for j in range(QKV_COLS // 128):
            if j % PROJ_GROUP == 0:
                wide = slice(j * 128, (j + PROJ_GROUP) * 128)
                y_wide = jnp.dot(xn, w_ref[:, wide], preferred_element_type=F32) + b_ref[:, wide]
            y = y_wide[:, (j % PROJ_GROUP) * 128:(j % PROJ_GROUP + 1) * 128]
            if j < n_rope:
                if kind == "a":
                    y = _rms(y, qg_ref[...] if j < n_q else kg_ref[...])
                    rot = pltpu.roll(y, 64, 1)
                else:
                    rot = jnp.where(first_half, pltpu.roll(y, 128 - C_DH // 2, 1), pltpu.roll(y, C_DH // 2, 1))
                y = y * cos_q + rot * sin_q if j < n_q else y * cos + rot * sin
            o_ref[rows, j * 128:(j + 1) * 128] = y.astype(BF16)


def _proj_attn(h, mods_l, g, w, b, q_gain, k_gain, cos_t, sin_t, kind):
    tpb = L // TM
    rope_idx = lambda i: (jnp.where(i < N_LAT_TILES, i % tpb, tpb), 0)
    assemble = isinstance(h, tuple)
    tile = pl.BlockSpec((TM, D), lambda i: (i, 0))
    if assemble:
        h_specs = [pl.BlockSpec((TM, D), lambda i: (jnp.minimum(i, N_LAT_TILES - 1), 0)),
                   pl.BlockSpec((TM, D), lambda i: (jnp.maximum(i - N_LAT_TILES, 0), 0))]
        h_args = list(h)
    else:
        h_specs, h_args = [tile], [h]
    qkv_spec = pl.BlockSpec((TM, QKV_COLS), lambda i: (i, 0))
    qkv_shape = jax.ShapeDtypeStruct((T, QKV_COLS), BF16)
    return pl.pallas_call(
        functools.partial(_proj_attn_kernel, kind=kind, assemble=assemble),
        grid=(N_TILES,),
        in_specs=h_specs + [
            pl.BlockSpec((None, 1, N_MOD * D), lambda i: (_mod_row(i), 0, 0)),
            pl.BlockSpec((4, D), lambda i: (0, 0)),
            pl.BlockSpec((D, QKV_COLS), lambda i: (0, 0)),
            pl.BlockSpec((1, QKV_COLS), lambda i: (0, 0)),
            pl.BlockSpec((1, 128), lambda i: (0, 0)),
            pl.BlockSpec((1, 128), lambda i: (0, 0)),
            pl.BlockSpec((TM, 128), rope_idx),
            pl.BlockSpec((TM, 128), rope_idx),
        ],
        out_specs=[qkv_spec, tile] if assemble else qkv_spec,
        out_shape=[qkv_shape, jax.ShapeDtypeStruct((T, D), F32)] if assemble else qkv_shape,
        compiler_params=_cparams(("parallel",), 40),
        name="proj_" + kind,
    )(*h_args, mods_l, g, w, b, q_gain, k_gain, cos_t, sin_t)


def _qk(q, k):
    return lax.dot_general(q, k, (((1,), (1,)), ((), ())), preferred_element_type=F32)


def _ones_ext(v):
    return jnp.concatenate([v, jnp.ones(v.shape, BF16)], axis=1)


def _attn_a_kernel(q_ref, kl_ref, vl_ref, kc_ref, vc_ref, o_ref, *, with_ctx):
    if with_ctx:
        is_ctx_step = pl.program_id(2) == L // TQA

        @pl.when(is_ctx_step)
        def _():
            _attn_a_body(q_ref, [[(kc_ref[...], vc_ref[...])]], o_ref)

        @pl.when(jnp.logical_not(is_ctx_step))
        def _():
            _attn_a_body(q_ref, _attn_a_latent_chunks(kl_ref, vl_ref, kc_ref, vc_ref), o_ref)
    else:
        _attn_a_body(q_ref, _attn_a_latent_chunks(kl_ref, vl_ref, kc_ref, vc_ref), o_ref)


def _attn_a_latent_chunks(kl_ref, vl_ref, kc_ref, vc_ref):
    chunks = [[(kc_ref[...], vc_ref[...]), (kl_ref[0:TK, :], vl_ref[0:TK, :])]]
    return chunks + [[(kl_ref[c * TK:(c + 1) * TK, :], vl_ref[c * TK:(c + 1) * TK, :])] for c in range(1, L // TK)]


def _attn_a_body(q_ref, chunks, o_ref):
    g = A_HEADS // A_KV
    m = [None] * g
    acc = [None] * g
    for ci, pieces in enumerate(chunks):
        v_exts = [_ones_ext(v) for _, v in pieces]
        scores = [jnp.concatenate([_qk(q_ref[:, i * A_DH:(i + 1) * A_DH], k) for k, _ in pieces], axis=1)
                  for i in range(g)]
        for i in range(g):
            s = scores[i]
            m_cur = jnp.broadcast_to(jnp.max(s, axis=-1, keepdims=True), (TQA, 128))
            m_new = m_cur if ci == 0 else jnp.maximum(m[i], m_cur)
            p = jnp.exp2(s - jnp.tile(m_new, (1, s.shape[1] // 128))).astype(BF16)
            pv, off = None, 0
            for (k, _), v_ext in zip(pieces, v_exts):
                part = jnp.dot(p[:, off:off + k.shape[0]], v_ext, preferred_element_type=F32)
                pv = part if pv is None else pv + part
                off += k.shape[0]
            if ci == 0:
                acc[i] = pv
            else:
                acc[i] = jnp.tile(jnp.exp2(m[i] - m_new), (1, 2)) * acc[i] + pv
            m[i] = m_new
    for i in range(g):
        o_ref[:, i * A_DH:(i + 1) * A_DH] = (acc[i][:, 0:A_DH] / acc[i][:, A_DH:]).astype(BF16)


def _attn_a(qkv, with_ctx):
    assert CTX == TQA
    nq = L // TQA
    gcols = (A_HEADS // A_KV) * A_DH // 128
    q_spec = pl.BlockSpec((TQA, gcols * 128),
                          lambda b, kv, j: (jnp.where(j < nq, b * nq + j, T_LAT // TQA + b), kv))
    return pl.pallas_call(
        functools.partial(_attn_a_kernel, with_ctx=with_ctx),
        grid=(B, A_KV, nq + with_ctx),
        in_specs=[
            q_spec,
            pl.BlockSpec((L, A_DH), lambda b, kv, j: (b, A_HEADS + kv)),
            pl.BlockSpec((L, A_DH), lambda b, kv, j: (b, A_HEADS + A_KV + kv)),
            pl.BlockSpec((CTX, A_DH), lambda b, kv, j: (T_LAT // CTX + b, A_HEADS + kv)),
            pl.BlockSpec((CTX, A_DH), lambda b, kv, j: (T_LAT // CTX + b, A_HEADS + A_KV + kv)),
        ],
        out_specs=q_spec,
        out_shape=jax.ShapeDtypeStruct((T if with_ctx else T_LAT, D), BF16),
        compiler_params=_cparams(("parallel", "parallel", "arbitrary"), 48),
        name="attn_a",
    )(qkv, qkv, qkv, qkv, qkv)


def _attn_c_kernel(q_ref, kl_ref, vl_ref, kx_ref, vx_ref, sink_ref, o_ref):
    j = pl.program_id(2)
    g = C_HEADS // C_KV
    lane = lax.broadcasted_iota(jnp.int32, (TQ, 128), 1)
    lo = lane < C_DH
    zero = jnp.zeros((TQ, 128), BF16)
    is_lat = j < NQ_LAT
    q0 = j * TQ
    win = TQ + 2 * WINDOW
    s0 = pl.multiple_of(jnp.clip(q0 - WINDOW, 0, L - win), 128)
    k_all = jnp.concatenate([kx_ref[...], kl_ref[pl.ds(s0, win), :]], axis=0)
    v_ext = _ones_ext(jnp.concatenate([vx_ref[...], vl_ref[pl.ds(s0, win), :]], axis=0))
    rel = (lax.broadcasted_iota(jnp.int32, (TQ, win), 1) - lax.broadcasted_iota(jnp.int32, (TQ, win), 0)
           + jnp.where(is_lat, s0 - q0, 4 * win))
    bias = jnp.concatenate([jnp.zeros((TQ, CTX), F32), jnp.where(jnp.abs(rel) <= WINDOW, 0.0, NEG_INF)], axis=1)
    heads = []
    for p in range(g // 2):
        qb = q_ref[:, p * 128:(p + 1) * 128]
        heads += [jnp.where(lo, qb, zero), jnp.where(lo, zero, qb)]
    scores = [_qk(qh, k_all) + bias for qh in heads]
    outs = []
    for h in range(g):
        s = scores[h]
        sink = jnp.tile(sink_ref[h * 8:(h + 1) * 8, :], (TQ // 8, 1))
        m_part = sink
        for c in range(s.shape[1] // 128):
            m_part = jnp.maximum(m_part, s[:, c * 128:(c + 1) * 128])
        m = jnp.broadcast_to(jnp.max(m_part, axis=-1, keepdims=True), (TQ, 128))
        p = jnp.exp2(s - jnp.tile(m, (1, s.shape[1] // 128))).astype(BF16)
        pv = jnp.dot(p, v_ext, preferred_element_type=F32)
        outs.append(pv[:, 0:128] / (pv[:, 128:] + jnp.exp2(sink - m)))
    for p in range(g // 2):
        o_ref[:, p * 128:(p + 1) * 128] = jnp.where(lo, outs[2 * p], outs[2 * p + 1]).astype(BF16)


def _attn_c(qkv, sink_rows):
    nq = NQ_LAT + NQ_CTX
    qcols = C_HEADS * C_DH // C_KV
    kcol = C_HEADS * C_DH // 128
    vcol = kcol + C_KV
    n_lat_blocks = T_LAT // TQ

    def q_idx(b, kv, j):
        return (jnp.where(j < NQ_LAT, b * NQ_LAT + j, n_lat_blocks + b * NQ_CTX + (j - NQ_LAT)), kv)

    lat = lambda col: pl.BlockSpec((L, 128), lambda b, kv, j: (b, col + kv))
    ctx = lambda col: pl.BlockSpec((CTX, 128), lambda b, kv, j: (T_LAT // CTX + b, col + kv))
    return pl.pallas_call(
        _attn_c_kernel,
        grid=(B, C_KV, nq),
        in_specs=[
            pl.BlockSpec((TQ, qcols), q_idx),
            lat(kcol), lat(vcol), ctx(kcol), ctx(vcol),
            pl.BlockSpec((None, (C_HEADS // C_KV) * 8, 128), lambda b, kv, j: (kv, 0, 0)),
        ],
        out_specs=pl.BlockSpec((TQ, qcols), q_idx),
        out_shape=jax.ShapeDtypeStruct((T, D), BF16),
        compiler_params=_cparams(("parallel", "parallel", "arbitrary"), 48),
        name="attn_c",
    )(qkv, qkv, qkv, qkv, qkv, sink_rows)


def _proj_b_kernel(h_ref, hp_ref, hn_ref, mod_ref, g_ref, w_ref, b_ref, cw_ref, cb_ref,
                   x0_ref, z_ref, xn_ref, u_ref):
    i = pl.program_id(0)
    m = mod_ref[...]
    _fill_xn(xn_ref, h_ref, hp_ref, hn_ref, g_ref[0:1, :], m[:, 0:D], m[:, D:2 * D])
    has_prev, has_next = _seg_masks(i)

    def matmul(k):
        u_ref[k % 2] = (jnp.dot(xn_ref[...], w_ref[:, k * D:(k + 1) * D], preferred_element_type=F32)
                        + b_ref[:, k * D:(k + 1) * D])

    matmul(0)
    x1 = None
    for k in range(3):
        if k + 1 < 3:
            matmul(k + 1)
        cols = slice(k * D, (k + 1) * D)
        conv = _conv3(u_ref.at[k % 2], cw_ref[:, cols], cb_ref[:, cols], has_prev, has_next, TM)
        if k == 0:
            x0_ref[...] = conv.astype(BF16)
        elif k == 1:
            x1 = conv
        else:
            z_ref[...] = (conv * x1).astype(BF16)


def _proj_b(h, mods_l, g, w, b, cw, cb):
    full = lambda shape: pl.BlockSpec(shape, lambda i: (0,) * len(shape))
    return pl.pallas_call(
        _proj_b_kernel,
        grid=(N_TILES,),
        in_specs=_row_specs(T) + [
            pl.BlockSpec((None, 1, N_MOD * D), lambda i: (_mod_row(i), 0, 0)),
            full((4, D)), full((D, 3 * D)), full((1, 3 * D)), full((3, 3 * D)), full((1, 3 * D)),
        ],
        out_specs=[pl.BlockSpec((TM, D), lambda i: (i, 0))] * 2,
        out_shape=[jax.ShapeDtypeStruct((T, D), BF16)] * 2,
        scratch_shapes=[pltpu.VMEM((TM + 2 * HALO, D), BF16), pltpu.VMEM((2, TM + 2 * HALO, D), F32)],
        compiler_params=_cparams(("parallel",), 48),
        name="proj_b",
    )(h, h, h, mods_l, g, w, b, cw, cb)


def _filter_kernel(zf_ref, t_ref, w1_ref, b1_ref, w2_ref, b2_ref, w3_ref, b3_ref, w4_ref, rates_ref, o_ref):
    i = pl.program_id(0)
    tm = zf_ref.shape[0]
    dot = functools.partial(jnp.dot, preferred_element_type=F32, precision=lax.Precision.HIGHEST)
    f = jnp.sin(FILTER_SIN_FREQ * (dot(zf_ref[...], w1_ref[...]) + b1_ref[...]))
    f = jnp.sin(FILTER_SIN_FREQ * (dot(f, w2_ref[...]) + b2_ref[...]))
    f = jnp.sin(FILTER_SIN_FREQ * (dot(f, w3_ref[...]) + b3_ref[...]))
    f = jnp.dot(f.astype(BF16), w4_ref[...].astype(BF16), preferred_element_type=F32)
    decay = jnp.exp(-t_ref[...] * rates_ref[...])
    row = i * tm + lax.broadcasted_iota(jnp.int32, (tm, 1), 0)
    o_ref[:, 0:D] = (f[:, 0:D] * decay).astype(BF16)
    o_ref[:, D:2 * D] = (f[:, D:2 * D] * decay * (row != 0).astype(F32)).astype(BF16)


def _hyena_filter(seq_len, w1, b1, w2, b2, w3, b3, w4):
    zf, t, rates = _filter_features(seq_len)
    tm = min(seq_len, 512)
    w1p = jnp.pad(w1, ((0, FILTER_WIDTH - FILTER_EMB), (0, 0)))
    full = lambda shape: pl.BlockSpec(shape, lambda i: (0,) * len(shape))
    fw = FILTER_WIDTH
    return pl.pallas_call(
        _filter_kernel,
        grid=(seq_len // tm,),
        in_specs=[
            pl.BlockSpec((tm, fw), lambda i: (i, 0)),
            pl.BlockSpec((tm, 1), lambda i: (i, 0)),
            full((fw, fw)), full((1, fw)), full((fw, fw)), full((1, fw)), full((fw, fw)), full((1, fw)),
            full((fw, 2 * D)), full((1, D)),
        ],
        out_specs=pl.BlockSpec((tm, 2 * D), lambda i: (i, 0)),
        out_shape=jax.ShapeDtypeStruct((seq_len, 2 * D), BF16),
        compiler_params=_cparams(("parallel",), 40),
        name="hyena_filter",
    )(zf, t, w1p, b1.reshape(1, fw), w2, b2.reshape(1, fw), w3, b3.reshape(1, fw), w4, rates)


def _dft_forward(x_ref, xs_ref, a_ref, m1_ref, tf_ref, emit):
    ct = x_ref.shape[1]
    xs_ref[...] = x_ref[...].astype(F32).reshape(FFT_H1, FFT_N2, ct)

    def stage1(g, carry):
        rows = pl.ds(pl.multiple_of(g * FFT_SUB, FFT_SUB), FFT_SUB)
        x = xs_ref[:, rows, :].reshape(FFT_H1 * FFT_SUB, ct).astype(BF16)
        a = jnp.dot(m1_ref[...], x, preferred_element_type=F32)
        a_ref[:, rows, :] = a.reshape(FFT_R, FFT_SUB, ct)
        return carry

    lax.fori_loop(0, FFT_N2 // FFT_SUB, stage1, 0, unroll=FFT_UNROLL1)

    def stage2(k1, carry):
        d = a_ref[pl.ds(2 * k1, 2)].reshape(2 * FFT_N2, ct).astype(BF16)
        emit(k1, jnp.dot(tf_ref[k1], d, preferred_element_type=F32))
        return carry

    lax.fori_loop(0, FFT_K1, stage2, 0, unroll=FFT_UNROLL)


def _filter_spectrum_kernel(hf_ref, hb_ref, m1_ref, tf_ref, o_ref, xs_ref, a_ref, f_ref):
    half = FFT_N2

    def keep(k1, x):
        f_ref[k1] = x

    _dft_forward(hf_ref, xs_ref, a_ref, m1_ref, tf_ref, keep)

    def combine(k1, x):
        f = f_ref[k1]
        o_ref[k1, 0:half, :] = (f[0:half] + x[0:half]).astype(BF16)
        o_ref[k1, half:, :] = (f[half:] - x[half:]).astype(BF16)

    _dft_forward(hb_ref, xs_ref, a_ref, m1_ref, tf_ref, combine)


def _long_conv_kernel(z_ref, hf_ref, m1_ref, tf_ref, ti_ref, m1i_ref, _, y_ref, xs_ref, a_ref, g_ref):
    half = FFT_N2
    ct = z_ref.shape[1]
    g_ref[2 * FFT_K1:] = jnp.zeros((FFT_R - 2 * FFT_K1, half, ct), F32)

    def product_and_invert(k1, x):
        hr = hf_ref[k1, 0:half, :].astype(F32)
        hi = hf_ref[k1, half:, :].astype(F32)
        xr, xi = x[0:half], x[half:]
        p = jnp.concatenate([xr * hr - xi * hi, xr * hi + xi * hr], axis=0).astype(BF16)
        g_ref[pl.ds(2 * k1, 2)] = jnp.dot(ti_ref[k1], p, preferred_element_type=F32).reshape(2, half, ct)

    _dft_forward(z_ref, xs_ref, a_ref, m1_ref, tf_ref, product_and_invert)

    def inverse_stage1(g, carry):
        rows = pl.ds(pl.multiple_of(g * FFT_SUB, FFT_SUB), FFT_SUB)
        x = g_ref[:, rows, :].reshape(FFT_R * FFT_SUB, ct).astype(BF16)
        y = jnp.dot(m1i_ref[...], x, preferred_element_type=F32)
        xs_ref[:, rows, :] = y.reshape(FFT_H1, FFT_SUB, ct)
        return carry

    lax.fori_loop(0, FFT_N2 // FFT_SUB, inverse_stage1, 0, unroll=FFT_UNROLL1)
    y_ref[...] = xs_ref[...].reshape(L, ct).astype(y_ref.dtype)


def _long_conv_latent(z, filt, spare):
    m1, m1i, tf, ti = _two_stage_dft()
    ct = FFT_CT
    n_ct = D // ct
    const = lambda shape: pl.BlockSpec(shape, lambda *_: (0,) * len(shape))
    scratch = [pltpu.VMEM((FFT_H1, FFT_N2, ct), F32), pltpu.VMEM((FFT_R, FFT_N2, ct), F32)]
    hf = pl.pallas_call(
        _filter_spectrum_kernel,
        grid=(n_ct,),
        in_specs=[
            pl.BlockSpec((L, ct), lambda c: (0, c)),
            pl.BlockSpec((L, ct), lambda c: (0, n_ct + c)),
            const(m1.shape), const(tf.shape),
        ],
        out_specs=pl.BlockSpec((FFT_K1, 2 * FFT_N2, ct), lambda c: (0, 0, c)),
        out_shape=jax.ShapeDtypeStruct((FFT_K1, 2 * FFT_N2, D), BF16),
        scratch_shapes=scratch + [pltpu.VMEM((FFT_K1, 2 * FFT_N2, ct), F32)],
        compiler_params=_cparams(("parallel",), 48),
        name="filter_spectrum",
    )(filt, filt, m1, tf)
    return pl.pallas_call(
        _long_conv_kernel,
        grid=(n_ct, B),
        in_specs=[
            pl.BlockSpec((L, ct), lambda c, b: (b, c)),
            pl.BlockSpec((FFT_K1, 2 * FFT_N2, ct), lambda c, b: (0, 0, c)),
            const(m1.shape), const(tf.shape), const(ti.shape), const(m1i.shape),
            pl.BlockSpec(memory_space=pl.ANY),
        ],
        out_specs=pl.BlockSpec((L, ct), lambda c, b: (b, c)),
        out_shape=jax.ShapeDtypeStruct((T, D), BF16),
        input_output_aliases={6: 0},
        scratch_shapes=scratch + [pltpu.VMEM((FFT_R, FFT_N2, ct), F32)],
        compiler_params=_cparams(("parallel", "arbitrary"), 56),
        name="long_conv",
    )(z, hf, m1, tf, ti, m1i, spare)


def _ctx_conv_kernel(z_ref, filt_ref, fwd_ref, inv_ref, _, y_ref):
    half = fwd_ref.shape[0] // 2
    xf = jnp.dot(fwd_ref[...], filt_ref[...], preferred_element_type=F32)
    hr = xf[0:half, 0:D] + xf[0:half, D:2 * D]
    hi = xf[half:, 0:D] - xf[half:, D:2 * D]
    x = jnp.dot(fwd_ref[...], z_ref[...], preferred_element_type=F32)
    xr, xi = x[0:half], x[half:]
    p = jnp.concatenate([xr * hr - xi * hi, xr * hi + xi * hr], axis=0).astype(BF16)
    y_ref[...] = jnp.dot(inv_ref[...], p, preferred_element_type=F32).astype(y_ref.dtype)


def _hyena_long_conv(z, filt_lat, filt_ctx, spare):
    y = _long_conv_latent(z, filt_lat, spare)
    fwd, inv = _direct_dft(CTX)
    full = lambda shape: pl.BlockSpec(shape, lambda b: (0,) * len(shape))
    ctx_rows = pl.BlockSpec((CTX, D), lambda b: (T_LAT // CTX + b, 0))
    return pl.pallas_call(
        _ctx_conv_kernel,
        grid=(B,),
        in_specs=[ctx_rows, full(filt_ctx.shape), full(fwd.shape), full(inv.shape),
                  pl.BlockSpec(memory_space=pl.ANY)],
        out_specs=ctx_rows,
        out_shape=jax.ShapeDtypeStruct((T, D), BF16),
        input_output_aliases={4: 0},
        compiler_params=_cparams(("parallel",), 48),
        name="ctx_conv",
    )(z, filt_ctx, fwd, inv, y)


def _layer_tail_kernel(*refs, hyena):
    h3, y3, refs = refs[0:3], refs[3:6], refs[6:]
    if hyena:
        z3, x03, skip_ref, refs = refs[0:3], refs[3:6], refs[6], refs[7:]
    mod_ref, g_ref, wo_ref, bo_ref, wup_ref, cw_ref, cb_ref, wd_ref, o_ref, xn_ref, gate_ref = refs
    i = pl.program_id(0)
    m = mod_ref[...]
    half = TM // 2
    ext_half = half + HALO

    def ext_rows(trio, s):
        prev_ref, main_ref, next_ref = trio[1], trio[0], trio[2]
        if s == 0:
            return jnp.concatenate([prev_ref[...], main_ref[0:half, :]], axis=0)
        return jnp.concatenate([main_ref[half:, :], next_ref[...]], axis=0)

    for s in range(2):
        mix = ext_rows(y3, s)
        if hyena:
            mix = ((mix.astype(F32) + ext_rows(z3, s).astype(F32) * skip_ref[...])
                   * ext_rows(x03, s).astype(F32)).astype(BF16)
        r = jnp.dot(mix, wo_ref[...], preferred_element_type=F32) + bo_ref[...]
        h_mid = ext_rows(h3, s) + m[:, 2 * D:3 * D] * _rms(r, g_ref[1:2, :])
        xn_ref[s * ext_half:(s + 1) * ext_half, :] = _norm_mod(
            h_mid, g_ref[2:3, :], m[:, 3 * D:4 * D], m[:, 4 * D:5 * D]).astype(BF16)
        o_ref[s * half:(s + 1) * half, :] = h_mid[HALO:, :] if s == 0 else h_mid[0:half, :]

    has_prev, has_next = _seg_masks(i)
    ext = SUB_ROWS + 2 * HALO
    n_sb = TM // SUB_ROWS
    units = [(c0, fc, sb) for c0, fc in FC_CHUNKS for sb in range(n_sb)]

    def up_matmuls(k):
        c0, fc, sb = units[k]
        r0 = sb * SUB_ROWS
        gate_ref[k % 2, :, 0:fc] = jnp.dot(xn_ref[r0:r0 + ext, :], wup_ref[:, c0:c0 + fc],
                                           preferred_element_type=F32)
        return jnp.dot(xn_ref[HALO + r0:HALO + r0 + SUB_ROWS, :],
                       wup_ref[:, D_FF + c0:D_FF + c0 + fc], preferred_element_type=F32)

    acc = [None] * n_sb
    up_next = up_matmuls(0)
    for k, (c0, fc, sb) in enumerate(units):
        up = up_next
        if k + 1 < len(units):
            up_next = up_matmuls(k + 1)
        cols = slice(c0, c0 + fc)
        rows = slice(sb * SUB_ROWS, (sb + 1) * SUB_ROWS)
        conv = _conv3(gate_ref.at[k % 2, :, 0:fc], cw_ref[:, cols], cb_ref[:, cols], has_prev[rows], has_next[rows],
                      SUB_ROWS)
        part = jnp.dot((_silu(conv) * up).astype(BF16), wd_ref[cols, :], preferred_element_type=F32)
        acc[sb] = part if c0 == 0 else acc[sb] + part
    for sb in range(n_sb):
        rows = slice(sb * SUB_ROWS, (sb + 1) * SUB_ROWS)
        o_ref[rows, :] = o_ref[rows, :] + m[:, 5 * D:6 * D] * _rms(acc[sb], g_ref[3:4, :])


def _layer_tail(h, y, mods_l, g, w_out, b_out, layer, w_up, cw, cb, w_down, n_tiles, hyena_args=None):
    resident = lambda shape: pl.BlockSpec(shape, lambda i: (0,) * len(shape), pipeline_mode=pl.Buffered(1))
    stacked = lambda shape: pl.BlockSpec((None,) + shape, lambda i: (layer,) + (0,) * len(shape),
                                         pipeline_mode=pl.Buffered(1))
    hyena = hyena_args is not None
    in_specs = _row_specs(h.shape[0]) + _row_specs(y.shape[0])
    args = [h, h, h, y, y, y]
    if hyena:
        z, x0, skip = hyena_args
        in_specs += _row_specs(z.shape[0]) + _row_specs(x0.shape[0]) + [resident((1, D))]
        args += [z, z, z, x0, x0, x0, skip]
    in_specs += [
        pl.BlockSpec((None, 1, N_MOD * D), lambda i: (_mod_row(i), 0, 0)),
        pl.BlockSpec((4, D), lambda i: (0, 0)),
        resident((D, D)), resident((1, D)),
        stacked((D, 2 * D_FF)), stacked((3, D_FF)), stacked((1, D_FF)), stacked((D_FF, D)),
    ]
    args += [mods_l, g, w_out, b_out, w_up, cw, cb.reshape(DEPTH, 1, D_FF), w_down]
    return pl.pallas_call(
        functools.partial(_layer_tail_kernel, hyena=hyena),
        grid=(n_tiles,),
        in_specs=in_specs,
        out_specs=pl.BlockSpec((TM, D), lambda i: (i, 0)),
        out_shape=jax.ShapeDtypeStruct((n_tiles * TM, D), F32),
        scratch_shapes=[
            pltpu.VMEM((TM + 2 * HALO, D), BF16),
            pltpu.VMEM((2, SUB_ROWS + 2 * HALO, FC), F32),
        ],
        compiler_params=_cparams(("parallel",), 56),
        name="layer_tail",
    )(*args)


def _dup_heads(w):
    k0, k1 = w[..., 0:C_DH], w[..., C_DH:2 * C_DH]
    return jnp.concatenate([k0, k0, k1, k1], axis=-1)


def kernel(x, c, ctx, c_ctx, mod_w, mod_b, norm_g, ffn_w_up, ffn_conv_w, ffn_conv_b, ffn_w_down, a_w_qkv, a_q_gain, a_k_gain, a_w_out, b_w_in, b_b_in, b_conv_w, b_conv_b, b_f_w1, b_f_b1, b_f_w2, b_f_b2, b_f_w3, b_f_b3, b_f_w4, b_skip, b_w_out, b_b_out, c_w_qkv, c_b_qkv, c_sink, c_w_out, c_b_out):
    h = (x.reshape(T_LAT, D), ctx.reshape(T_CTX, D))
    cs = jnp.concatenate([c, c_ctx[None], jnp.zeros((8 - B - 1, D), F32)], axis=0)
    mods = _mods(cs, mod_w, mod_b).reshape(DEPTH, 8, 1, N_MOD * D)
    cos_a, sin_a = _rope_tables(A_DH)
    cos_c, sin_c = _rope_tables(C_DH)
    zeros_d = jnp.zeros((1, D), F32)
    w_up_bf16, w_down_bf16 = ffn_w_up.astype(BF16), ffn_w_down.astype(BF16)
    y = None

    for i in range(DEPTH):
        last = i == DEPTH - 1
        kind, j = i % 3, i // 3
        g = norm_g[i]
        mods_l = mods[i]
        n_tiles = N_LAT_TILES if last else N_TILES
        if kind == 0:
            qkv = _proj_attn(h, mods_l, g, a_w_qkv[j].astype(BF16), jnp.zeros((1, QKV_COLS), F32),
                             a_q_gain[j].reshape(1, A_DH), a_k_gain[j].reshape(1, A_DH), cos_a, sin_a, "a")
            if isinstance(h, tuple):
                qkv, h = qkv
            y = _attn_a(qkv, with_ctx=not last)
            w_out, b_out, hyena_args = a_w_out[j], zeros_d, None
        elif kind == 1:
            x0, z = _proj_b(h, mods_l, g, b_w_in[j].astype(BF16), b_b_in[j].reshape(1, 3 * D),
                            b_conv_w[j], b_conv_b[j].reshape(1, 3 * D))
            fw = (b_f_w1[j], b_f_b1[j], b_f_w2[j], b_f_b2[j], b_f_w3[j], b_f_b3[j], b_f_w4[j])
            spare = y if y is not None and y.shape == (T, D) else jnp.zeros((T, D), BF16)
            y = _hyena_long_conv(z, _hyena_filter(L, *fw), _hyena_filter(CTX, *fw), spare)
            w_out, b_out = b_w_out[j], b_b_out[j].reshape(1, D)
            hyena_args = (z, x0, b_skip[j].reshape(1, D))
        else:
            qc = C_HEADS * C_DH
            kc = qc + C_KV * C_DH
            w, bias = c_w_qkv[j], c_b_qkv[j].reshape(1, -1)
            w = jnp.concatenate([w[:, :qc], _dup_heads(w[:, qc:kc]), _dup_heads(w[:, kc:])], axis=1)
            bias = jnp.concatenate([bias[:, :qc], _dup_heads(bias[:, qc:kc]), _dup_heads(bias[:, kc:])], axis=1)
            ones = jnp.ones((1, 128), F32)
            qkv = _proj_attn(h, mods_l, g, w.astype(BF16), bias, ones, ones, cos_c, sin_c, "c")
            sink_rows = jnp.broadcast_to((c_sink[j].astype(F32) * LOG2_E)[:, None, None],
                                         (C_HEADS, 8, 128)).reshape(C_KV, (C_HEADS // C_KV) * 8, 128)
            y = _attn_c(qkv, sink_rows)
            w_out, b_out, hyena_args = c_w_out[j], c_b_out[j].reshape(1, D), None
        h = _layer_tail(h, y, mods_l, g, w_out.astype(BF16), b_out, i, w_up_bf16, ffn_conv_w, ffn_conv_b,
                        w_down_bf16, n_tiles, hyena_args)
    return h.reshape(B, L, D)
```

```python
import functools
import math

import numpy as np
import jax
import jax.numpy as jnp
from jax import lax
from jax.experimental import pallas as pl
from jax.experimental.pallas import tpu as pltpu

F32 = jnp.float32
BF16 = jnp.bfloat16

D = 1024
B = 4
L = 4096
CTX = 256
DEPTH = 4
N_MOD = 6
EPS = 1e-6
NEG_INF = -1e30
LOG2_E = math.log2(math.e)
GRID_W = 64
ROPE_THETA = 10000.0

A_HEADS, A_KV, A_DH = 8, 2, 128
C_HEADS, C_KV, C_DH = 16, 2, 64
QKV_COLS = 1536
PROJ_GROUP = 4
SUB_ROWS = 256
PROJ_ROWS = 256
D_FF = 2816
FILTER_BANDS = 16
FILTER_EMB = 1 + 2 * FILTER_BANDS
FILTER_WIDTH = 64
FILTER_SIN_FREQ = 1.0
DECAY_TARGET = 1e-2
DECAY_FAST_PCT = 0.3
DECAY_SLOW_PCT = 1.5

T_LAT = B * L
T_CTX = B * CTX
T = T_LAT + T_CTX

TM = 512
HALO = 16
N_LAT_TILES = T_LAT // TM
N_TILES = T // TM
MXU_W = 256
FC_CHUNKS = ((0, 6 * MXU_W), (6 * MXU_W, 5 * MXU_W))
FC = max(w for _, w in FC_CHUNKS)
WINDOW = 128
TQ = 256
TQA = 256
NQ_LAT = L // TQ
NQ_CTX = CTX // TQ

FFT_N = 2 * L
FFT_N1 = 64
FFT_N2 = FFT_N // FFT_N1
FFT_H1 = FFT_N1 // 2
FFT_K1 = FFT_H1 + 1
FFT_R = 72
FFT_CT = 256
FFT_SUB = 8
FFT_UNROLL = 33
FFT_UNROLL1 = 8

VMEM_MB = 1024 * 1024


def _cparams(sem, vmem_mb):
    return pltpu.CompilerParams(dimension_semantics=sem, vmem_limit_bytes=vmem_mb * VMEM_MB)


def _rope_tables(head_dim):
    rows = L // GRID_W
    row = np.repeat(np.arange(rows), GRID_W).astype(np.float64)
    col = np.tile(np.arange(GRID_W), rows).astype(np.float64)
    axis_dim = head_dim // 2
    inv = np.power(ROPE_THETA, -np.arange(0, axis_dim, 2, dtype=np.float64) / axis_dim)
    ang = np.concatenate([row[:, None] * inv[None], col[:, None] * inv[None]], axis=-1)
    cos, sin = np.cos(ang), np.sin(ang)
    reps = 128 // head_dim
    cos_t = np.tile(np.concatenate([cos, cos], axis=-1), (1, reps))
    sin_t = np.tile(np.concatenate([-sin, sin], axis=-1), (1, reps))
    cos_t = np.concatenate([cos_t, np.ones((TM, 128))], axis=0)
    sin_t = np.concatenate([sin_t, np.zeros((TM, 128))], axis=0)
    return jnp.asarray(cos_t, F32), jnp.asarray(sin_t, F32)


def _two_stage_dft():
    n1s, n2s, h1, k1n, n = FFT_N1, FFT_N2, FFT_H1, FFT_K1, FFT_N
    n1 = np.arange(h1)
    k1 = np.arange(k1n)
    ang = 2 * np.pi * np.outer(k1, n1) / n1s
    m1 = np.zeros((FFT_R, h1))
    m1[0:2 * k1n:2] = np.cos(ang)
    m1[1:2 * k1n:2] = -np.sin(ang)
    c = np.where((k1 == 0) | (k1 == h1), 1.0, 2.0)
    m1i = np.zeros((h1, FFT_R))
    m1i[:, 0:2 * k1n:2] = (c[None] / n) * np.cos(ang.T)
    m1i[:, 1:2 * k1n:2] = -(c[None] / n) * np.sin(ang.T)
    n2 = np.arange(n2s)
    k2 = np.arange(n2s)
    tf = np.zeros((k1n, 2 * n2s, 2 * n2s))
    ti = np.zeros((k1n, 2 * n2s, 2 * n2s))
    for a in range(k1n):
        th = 2 * np.pi * np.outer(a + n1s * k2, n2) / n
        er, ei = np.cos(th), -np.sin(th)
        tf[a] = np.block([[er, -ei], [ei, er]])
        er, ei = np.cos(th.T), np.sin(th.T)
        ti[a] = np.block([[er, -ei], [ei, er]])
    eye = np.eye(FFT_SUB)
    return tuple(jnp.asarray(a, F32).astype(BF16) for a in (np.kron(m1, eye), np.kron(m1i, eye), tf, ti))


def _direct_dft(seq_len):
    n = 2 * seq_len
    th = 2 * np.pi * np.outer(np.arange(n), np.arange(seq_len)) / n
    fwd = np.concatenate([np.cos(th), -np.sin(th)], axis=0)
    inv = np.concatenate([np.cos(th.T), -np.sin(th.T)], axis=1) / n
    return jnp.asarray(fwd, F32).astype(BF16), jnp.asarray(inv, F32).astype(BF16)


def _filter_features(seq_len):
    t = np.linspace(0.0, 1.0, seq_len)[:, None]
    omega = (2.0 * math.pi / seq_len) * np.arange(seq_len, dtype=np.float64)
    bands = np.linspace(1e-4, FILTER_BANDS - 1, FILTER_BANDS)
    ang = omega[:, None] * bands[None, :]
    z = np.concatenate([t, np.cos(ang), -np.sin(ang)], axis=-1)
    z = np.pad(z, ((0, 0), (0, FILTER_WIDTH - FILTER_EMB)))
    rates = np.abs(np.linspace(math.log(DECAY_TARGET) / DECAY_FAST_PCT,
                               math.log(DECAY_TARGET) / DECAY_SLOW_PCT, D))[None]
    return jnp.asarray(z, F32), jnp.asarray(t, F32), jnp.asarray(rates, F32)


def _rms(x, g):
    return x * lax.rsqrt(jnp.mean(x * x, axis=-1, keepdims=True) + EPS) * g


def _norm_mod(h, g, shift, scale):
    return _rms(h, g) * (1.0 + scale) + shift


def _silu(x):
    return x * (1.0 / (1.0 + jnp.exp(-x)))


def _seg_masks(i):
    rows = i * TM + lax.broadcasted_iota(jnp.int32, (TM, 1), 0)
    seg = jnp.where(i < N_LAT_TILES, L, CTX)
    pos = rows & (seg - 1)
    return (pos != 0).astype(F32), (pos != seg - 1).astype(F32)


def _fill_xn(xn_ref, h_ref, hp_ref, hn_ref, g, shift, scale):
    xn_ref[0:HALO, :] = _norm_mod(hp_ref[...], g, shift, scale).astype(BF16)
    xn_ref[HALO:HALO + TM, :] = _norm_mod(h_ref[...], g, shift, scale).astype(BF16)
    xn_ref[HALO + TM:, :] = _norm_mod(hn_ref[...], g, shift, scale).astype(BF16)


def _conv3(u_ref, cw, cb, has_prev, has_next, n_rows):
    u = u_ref[...]
    total = n_rows + 2 * HALO
    u_prev = pltpu.roll(u, 1, 0)
    u_next = pltpu.roll(u, total - 1, 0)

    def piece(r0, r1, masked):
        prev = u_prev[HALO + r0:HALO + r1]
        nxt = u_next[HALO + r0:HALO + r1]
        if masked:
            prev = prev * has_prev[r0:r1]
            nxt = nxt * has_next[r0:r1]
        return cb + cw[0:1] * prev + cw[1:2] * u[HALO + r0:HALO + r1] + cw[2:3] * nxt

    pieces = []
    for s in range(0, n_rows, CTX):
        pieces += [piece(s, s + 8, True), piece(s + 8, s + CTX - 8, False), piece(s + CTX - 8, s + CTX, True)]
    return jnp.concatenate(pieces, axis=0)


def _mod_row(i):
    return jnp.where(i < N_LAT_TILES, i // (L // TM), B)


def _row_specs(in_rows):
    last = in_rows // HALO - 1
    per = TM // HALO
    return [
        pl.BlockSpec((TM, D), lambda i, *_: (i, 0)),
        pl.BlockSpec((HALO, D), lambda i, *_: (jnp.maximum(i * per - 1, 0), 0)),
        pl.BlockSpec((HALO, D), lambda i, *_: (jnp.minimum((i + 1) * per, last), 0)),
    ]


def _mods_kernel(s_ref, w_ref, b_ref, o_ref):
    s = _silu(s_ref[...])
    o_ref[...] = jnp.dot(s, w_ref[...], preferred_element_type=F32,
                         precision=lax.Precision.HIGHEST) + b_ref[...]


def _mods(cs, mod_w, mod_b):
    tn = 1536
    return pl.pallas_call(
        _mods_kernel,
        grid=(DEPTH, N_MOD * D // tn),
        in_specs=[
            pl.BlockSpec((8, D), lambda l, n: (0, 0)),
            pl.BlockSpec((None, D, tn), lambda l, n: (l, 0, n)),
            pl.BlockSpec((None, 1, tn), lambda l, n: (l, 0, n)),
        ],
        out_specs=pl.BlockSpec((None, 8, tn), lambda l, n: (l, 0, n)),
        out_shape=jax.ShapeDtypeStruct((DEPTH, 8, N_MOD * D), F32),
        compiler_params=_cparams(("parallel", "parallel"), 40),
        name="mods",
    )(cs, mod_w, mod_b.reshape(DEPTH, 1, N_MOD * D))


def _proj_attn_kernel(*refs, kind, assemble):
    if assemble:
        lat_ref, ctx_ref, refs, h_out_ref = refs[0], refs[1], refs[2:-1], refs[-1]
        is_lat = pl.program_id(0) < N_LAT_TILES
    else:
        h_ref, refs = refs[0], refs[1:]
    mod_ref, g_ref, w_ref, b_ref, qg_ref, kg_ref, cos_ref, sin_ref, o_ref = refs
    m = mod_ref[...]
    n_q = 8
    n_rope = 10
    q_scale = (A_DH if kind == "a" else C_DH) ** -0.5 * LOG2_E
    if kind == "c":
        lane = lax.broadcasted_iota(jnp.int32, (PROJ_ROWS, 128), 1)
        first_half = (lane & (C_DH - 1)) < C_DH // 2
    for r0 in range(0, TM, PROJ_ROWS):
        rows = slice(r0, r0 + PROJ_ROWS)
        if assemble:
            h = jnp.where(is_lat, lat_ref[rows, :], ctx_ref[rows, :])
            h_out_ref[rows, :] = h
        else:
            h = h_ref[rows, :]
        xn = _norm_mod(h, g_ref[0:1, :], m[:, 0:D], m[:, D:2 * D]).astype(BF16)
        cos, sin = cos_ref[rows, :], sin_ref[rows, :]
        cos_q, sin_q = cos * q_scale, sin * q_scale
        for j in range(QKV_COLS // 128):
            if j % PROJ_GROUP == 0:
                wide = slice(j * 128, (j + PROJ_GROUP) * 128)
                y_wide = jnp.dot(xn, w_ref[:, wide], preferred_element_type=F32) + b_ref[:, wide]
            y = y_wide[:, (j % PROJ_GROUP) * 128:(j % PROJ_GROUP + 1) * 128]
            if j < n_rope:
                if kind == "a":
                    y = _rms(y, qg_ref[...] if j < n_q else kg_ref[...])
                    rot = pltpu.roll(y, 64, 1)
                else:
                    rot = jnp.where(first_half, pltpu.roll(y, 128 - C_DH // 2, 1), pltpu.roll(y, C_DH // 2, 1))
                y = y * cos_q + rot * sin_q if j < n_q else y * cos + rot * sin
            o_ref[rows, j * 128:(j + 1) * 128] = y.astype(BF16)


def _proj_attn(h, mods_l, g, w, b, q_gain, k_gain, cos_t, sin_t, kind):
    tpb = L // TM
    rope_idx = lambda i: (jnp.where(i < N_LAT_TILES, i % tpb, tpb), 0)
    assemble = isinstance(h, tuple)
    tile = pl.BlockSpec((TM, D), lambda i: (i, 0))
    if assemble:
        h_specs = [pl.BlockSpec((TM, D), lambda i: (jnp.minimum(i, N_LAT_TILES - 1), 0)),
                   pl.BlockSpec((TM, D), lambda i: (jnp.maximum(i - N_LAT_TILES, 0), 0))]
        h_args = list(h)
    else:
        h_specs, h_args = [tile], [h]
    qkv_spec = pl.BlockSpec((TM, QKV_COLS), lambda i: (i, 0))
    qkv_shape = jax.ShapeDtypeStruct((T, QKV_COLS), BF16)
    return pl.pallas_call(
        functools.partial(_proj_attn_kernel, kind=kind, assemble=assemble),
        grid=(N_TILES,),
        in_specs=h_specs + [
            pl.BlockSpec((None, 1, N_MOD * D), lambda i: (_mod_row(i), 0, 0)),
            pl.BlockSpec((4, D), lambda i: (0, 0)),
            pl.BlockSpec((D, QKV_COLS), lambda i: (0, 0)),
            pl.BlockSpec((1, QKV_COLS), lambda i: (0, 0)),
            pl.BlockSpec((1, 128), lambda i: (0, 0)),
            pl.BlockSpec((1, 128), lambda i: (0, 0)),
            pl.BlockSpec((TM, 128), rope_idx),
            pl.BlockSpec((TM, 128), rope_idx),
        ],
        out_specs=[qkv_spec, tile] if assemble else qkv_spec,
        out_shape=[qkv_shape, jax.ShapeDtypeStruct((T, D), F32)] if assemble else qkv_shape,
        compiler_params=_cparams(("parallel",), 40),
        name="proj_" + kind,
    )(*h_args, mods_l, g, w, b, q_gain, k_gain, cos_t, sin_t)


def _qk(q, k):
    return lax.dot_general(q, k, (((1,), (1,)), ((), ())), preferred_element_type=F32)


def _ones_ext(v):
    return jnp.concatenate([v, jnp.ones(v.shape, BF16)], axis=1)


def _attn_a_kernel(q_ref, kl_ref, vl_ref, kc_ref, vc_ref, o_ref, *, with_ctx):
    if with_ctx:
        is_ctx_step = pl.program_id(2) == L // TQA

        @pl.when(is_ctx_step)
        def _():
            _attn_a_body(q_ref, [(kc_ref[...], vc_ref[...])], o_ref)

        @pl.when(jnp.logical_not(is_ctx_step))
        def _():
            _attn_a_body(q_ref, [(kc_ref[...], vc_ref[...]), (kl_ref[...], vl_ref[...])], o_ref)
    else:
        _attn_a_body(q_ref, [(kc_ref[...], vc_ref[...]), (kl_ref[...], vl_ref[...])], o_ref)


def _attn_a_body(q_ref, pieces, o_ref):
    g = A_HEADS // A_KV
    v_exts = [_ones_ext(v) for _, v in pieces]
    scores = [jnp.concatenate([_qk(q_ref[:, i * A_DH:(i + 1) * A_DH], k) for k, _ in pieces], axis=1)
              for i in range(g)]
    for i in range(g):
        s = scores[i]
        m = jnp.broadcast_to(jnp.max(s, axis=-1, keepdims=True), (TQA, 128))
        p = jnp.exp2(s - jnp.tile(m, (1, s.shape[1] // 128))).astype(BF16)
        pv, off = None, 0
        for (k, _), v_ext in zip(pieces, v_exts):
            part = jnp.dot(p[:, off:off + k.shape[0]], v_ext, preferred_element_type=F32)
            pv = part if pv is None else pv + part
            off += k.shape[0]
        o_ref[:, i * A_DH:(i + 1) * A_DH] = (pv[:, 0:A_DH] / pv[:, A_DH:]).astype(BF16)


def _attn_a(qkv, with_ctx):
    assert CTX == TQA
    nq = L // TQA
    gcols = (A_HEADS // A_KV) * A_DH // 128
    q_spec = pl.BlockSpec((TQA, gcols * 128),
                          lambda b, kv, j: (jnp.where(j < nq, b * nq + j, T_LAT // TQA + b), kv))
    return pl.pallas_call(
        functools.partial(_attn_a_kernel, with_ctx=with_ctx),
        grid=(B, A_KV, nq + with_ctx),
        in_specs=[
            q_spec,
            pl.BlockSpec((L, A_DH), lambda b, kv, j: (b, A_HEADS + kv)),
            pl.BlockSpec((L, A_DH), lambda b, kv, j: (b, A_HEADS + A_KV + kv)),
            pl.BlockSpec((CTX, A_DH), lambda b, kv, j: (T_LAT // CTX + b, A_HEADS + kv)),
            pl.BlockSpec((CTX, A_DH), lambda b, kv, j: (T_LAT // CTX + b, A_HEADS + A_KV + kv)),
        ],
        out_specs=q_spec,
        out_shape=jax.ShapeDtypeStruct((T if with_ctx else T_LAT, D), BF16),
        compiler_params=_cparams(("parallel", "parallel", "arbitrary"), 48),
        name="attn_a",
    )(qkv, qkv, qkv, qkv, qkv)


def _attn_c_kernel(q_ref, kl_ref, vl_ref, kx_ref, vx_ref, sink_ref, o_ref):
    j = pl.program_id(2)
    g = C_HEADS // C_KV
    lane = lax.broadcasted_iota(jnp.int32, (TQ, 128), 1)
    lo = lane < C_DH
    zero = jnp.zeros((TQ, 128), BF16)
    is_lat = j < NQ_LAT
    q0 = j * TQ
    win = TQ + 2 * WINDOW
    s0 = pl.multiple_of(jnp.clip(q0 - WINDOW, 0, L - win), 128)
    k_all = jnp.concatenate([kx_ref[...], kl_ref[pl.ds(s0, win), :]], axis=0)
    v_ext = _ones_ext(jnp.concatenate([vx_ref[...], vl_ref[pl.ds(s0, win), :]], axis=0))
    rel = (lax.broadcasted_iota(jnp.int32, (TQ, win), 1) - lax.broadcasted_iota(jnp.int32, (TQ, win), 0)
           + jnp.where(is_lat, s0 - q0, 4 * win))
    bias = jnp.concatenate([jnp.zeros((TQ, CTX), F32), jnp.where(jnp.abs(rel) <= WINDOW, 0.0, NEG_INF)], axis=1)
    heads = []
    for p in range(g // 2):
        qb = q_ref[:, p * 128:(p + 1) * 128]
        heads += [jnp.where(lo, qb, zero), jnp.where(lo, zero, qb)]
    scores = [_qk(qh, k_all) + bias for qh in heads]
    outs = []
    for h in range(g):
        s = scores[h]
        sink = jnp.tile(sink_ref[h * 8:(h + 1) * 8, :], (TQ // 8, 1))
        m_part = sink
        for c in range(s.shape[1] // 128):
            m_part = jnp.maximum(m_part, s[:, c * 128:(c + 1) * 128])
        m = jnp.broadcast_to(jnp.max(m_part, axis=-1, keepdims=True), (TQ, 128))
        p = jnp.exp2(s - jnp.tile(m, (1, s.shape[1] // 128))).astype(BF16)
        pv = jnp.dot(p, v_ext, preferred_element_type=F32)
        outs.append(pv[:, 0:128] / (pv[:, 128:] + jnp.exp2(sink - m)))
    for p in range(g // 2):
        o_ref[:, p * 128:(p + 1) * 128] = jnp.where(lo, outs[2 * p], outs[2 * p + 1]).astype(BF16)


def _attn_c(qkv, sink_rows):
    nq = NQ_LAT + NQ_CTX
    qcols = C_HEADS * C_DH // C_KV
    kcol = C_HEADS * C_DH // 128
    vcol = kcol + C_KV
    n_lat_blocks = T_LAT // TQ

    def q_idx(b, kv, j):
        return (jnp.where(j < NQ_LAT, b * NQ_LAT + j, n_lat_blocks + b * NQ_CTX + (j - NQ_LAT)), kv)

    lat = lambda col: pl.BlockSpec((L, 128), lambda b, kv, j: (b, col + kv))
    ctx = lambda col: pl.BlockSpec((CTX, 128), lambda b, kv, j: (T_LAT // CTX + b, col + kv))
    return pl.pallas_call(
        _attn_c_kernel,
        grid=(B, C_KV, nq),
        in_specs=[
            pl.BlockSpec((TQ, qcols), q_idx),
            lat(kcol), lat(vcol), ctx(kcol), ctx(vcol),
            pl.BlockSpec((None, (C_HEADS // C_KV) * 8, 128), lambda b, kv, j: (kv, 0, 0)),
        ],
        out_specs=pl.BlockSpec((TQ, qcols), q_idx),
        out_shape=jax.ShapeDtypeStruct((T, D), BF16),
        compiler_params=_cparams(("parallel", "parallel", "arbitrary"), 48),
        name="attn_c",
    )(qkv, qkv, qkv, qkv, qkv, sink_rows)


def _proj_b_kernel(h_ref, hp_ref, hn_ref, mod_ref, g_ref, w_ref, b_ref, cw_ref, cb_ref,
                   x0_ref, z_ref, xn_ref, u_ref):
    i = pl.program_id(0)
    m = mod_ref[...]
    _fill_xn(xn_ref, h_ref, hp_ref, hn_ref, g_ref[0:1, :], m[:, 0:D], m[:, D:2 * D])
    has_prev, has_next = _seg_masks(i)

    def matmul(k):
        u_ref[k % 2] = (jnp.dot(xn_ref[...], w_ref[:, k * D:(k + 1) * D], preferred_element_type=F32)
                        + b_ref[:, k * D:(k + 1) * D])

    matmul(0)
    x1 = None
    for k in range(3):
        if k + 1 < 3:
            matmul(k + 1)
        cols = slice(k * D, (k + 1) * D)
        conv = _conv3(u_ref.at[k % 2], cw_ref[:, cols], cb_ref[:, cols], has_prev, has_next, TM)
        if k == 0:
            x0_ref[...] = conv.astype(BF16)
        elif k == 1:
            x1 = conv
        else:
            z_ref[...] = (conv * x1).astype(BF16)


def _proj_b(h, mods_l, g, w, b, cw, cb):
    full = lambda shape: pl.BlockSpec(shape, lambda i: (0,) * len(shape))
    return pl.pallas_call(
        _proj_b_kernel,
        grid=(N_TILES,),
        in_specs=_row_specs(T) + [
            pl.BlockSpec((None, 1, N_MOD * D), lambda i: (_mod_row(i), 0, 0)),
            full((4, D)), full((D, 3 * D)), full((1, 3 * D)), full((3, 3 * D)), full((1, 3 * D)),
        ],
        out_specs=[pl.BlockSpec((TM, D), lambda i: (i, 0))] * 2,
        out_shape=[jax.ShapeDtypeStruct((T, D), BF16)] * 2,
        scratch_shapes=[pltpu.VMEM((TM + 2 * HALO, D), BF16), pltpu.VMEM((2, TM + 2 * HALO, D), F32)],
        compiler_params=_cparams(("parallel",), 48),
        name="proj_b",
    )(h, h, h, mods_l, g, w, b, cw, cb)


def _filter_kernel(zf_ref, t_ref, w1_ref, b1_ref, w2_ref, b2_ref, w3_ref, b3_ref, w4_ref, rates_ref, o_ref):
    i = pl.program_id(0)
    tm = zf_ref.shape[0]
    dot = functools.partial(jnp.dot, preferred_element_type=F32, precision=lax.Precision.HIGHEST)
    f = jnp.sin(FILTER_SIN_FREQ * (dot(zf_ref[...], w1_ref[...]) + b1_ref[...]))
    f = jnp.sin(FILTER_SIN_FREQ * (dot(f, w2_ref[...]) + b2_ref[...]))
    f = jnp.sin(FILTER_SIN_FREQ * (dot(f, w3_ref[...]) + b3_ref[...]))
    f = jnp.dot(f.astype(BF16), w4_ref[...].astype(BF16), preferred_element_type=F32)
    decay = jnp.exp(-t_ref[...] * rates_ref[...])
    row = i * tm + lax.broadcasted_iota(jnp.int32, (tm, 1), 0)
    o_ref[:, 0:D] = (f[:, 0:D] * decay).astype(BF16)
    o_ref[:, D:2 * D] = (f[:, D:2 * D] * decay * (row != 0).astype(F32)).astype(BF16)


def _hyena_filter(seq_len, w1, b1, w2, b2, w3, b3, w4):
    zf, t, rates = _filter_features(seq_len)
    tm = min(seq_len, 512)
    w1p = jnp.pad(w1, ((0, FILTER_WIDTH - FILTER_EMB), (0, 0)))
    full = lambda shape: pl.BlockSpec(shape, lambda i: (0,) * len(shape))
    fw = FILTER_WIDTH
    return pl.pallas_call(
        _filter_kernel,
        grid=(seq_len // tm,),
        in_specs=[
            pl.BlockSpec((tm, fw), lambda i: (i, 0)),
            pl.BlockSpec((tm, 1), lambda i: (i, 0)),
            full((fw, fw)), full((1, fw)), full((fw, fw)), full((1, fw)), full((fw, fw)), full((1, fw)),
            full((fw, 2 * D)), full((1, D)),
        ],
        out_specs=pl.BlockSpec((tm, 2 * D), lambda i: (i, 0)),
        out_shape=jax.ShapeDtypeStruct((seq_len, 2 * D), BF16),
        compiler_params=_cparams(("parallel",), 40),
        name="hyena_filter",
    )(zf, t, w1p, b1.reshape(1, fw), w2, b2.reshape(1, fw), w3, b3.reshape(1, fw), w4, rates)


def _dft_forward(x_ref, xs_ref, a_ref, m1_ref, tf_ref, emit):
    ct = x_ref.shape[1]
    xs_ref[...] = x_ref[...].astype(F32).reshape(FFT_H1, FFT_N2, ct)

    def stage1(g, carry):
        rows = pl.ds(pl.multiple_of(g * FFT_SUB, FFT_SUB), FFT_SUB)
        x = xs_ref[:, rows, :].reshape(FFT_H1 * FFT_SUB, ct).astype(BF16)
        a = jnp.dot(m1_ref[...], x, preferred_element_type=F32)
        a_ref[:, rows, :] = a.reshape(FFT_R, FFT_SUB, ct)
        return carry

    lax.fori_loop(0, FFT_N2 // FFT_SUB, stage1, 0, unroll=FFT_UNROLL1)

    def stage2(k1, carry):
        d = a_ref[pl.ds(2 * k1, 2)].reshape(2 * FFT_N2, ct).astype(BF16)
        emit(k1, jnp.dot(tf_ref[k1], d, preferred_element_type=F32))
        return carry

    lax.fori_loop(0, FFT_K1, stage2, 0, unroll=FFT_UNROLL)


def _filter_spectrum_kernel(hf_ref, hb_ref, m1_ref, tf_ref, o_ref, xs_ref, a_ref, f_ref):
    half = FFT_N2

    def keep(k1, x):
        f_ref[k1] = x

    _dft_forward(hf_ref, xs_ref, a_ref, m1_ref, tf_ref, keep)

    def combine(k1, x):
        f = f_ref[k1]
        o_ref[k1, 0:half, :] = (f[0:half] + x[0:half]).astype(BF16)
        o_ref[k1, half:, :] = (f[half:] - x[half:]).astype(BF16)

    _dft_forward(hb_ref, xs_ref, a_ref, m1_ref, tf_ref, combine)


def _long_conv_kernel(z_ref, hf_ref, m1_ref, tf_ref, ti_ref, m1i_ref, _, y_ref, xs_ref, a_ref, g_ref):
    half = FFT_N2
    ct = z_ref.shape[1]
    g_ref[2 * FFT_K1:] = jnp.zeros((FFT_R - 2 * FFT_K1, half, ct), F32)

    def product_and_invert(k1, x):
        hr = hf_ref[k1, 0:half, :].astype(F32)
        hi = hf_ref[k1, half:, :].astype(F32)
        xr, xi = x[0:half], x[half:]
        p = jnp.concatenate([xr * hr - xi * hi, xr * hi + xi * hr], axis=0).astype(BF16)
        g_ref[pl.ds(2 * k1, 2)] = jnp.dot(ti_ref[k1], p, preferred_element_type=F32).reshape(2, half, ct)

    _dft_forward(z_ref, xs_ref, a_ref, m1_ref, tf_ref, product_and_invert)

    def inverse_stage1(g, carry):
        rows = pl.ds(pl.multiple_of(g * FFT_SUB, FFT_SUB), FFT_SUB)
        x = g_ref[:, rows, :].reshape(FFT_R * FFT_SUB, ct).astype(BF16)
        y = jnp.dot(m1i_ref[...], x, preferred_element_type=F32)
        xs_ref[:, rows, :] = y.reshape(FFT_H1, FFT_SUB, ct)
        return carry

    lax.fori_loop(0, FFT_N2 // FFT_SUB, inverse_stage1, 0, unroll=FFT_UNROLL1)
    y_ref[...] = xs_ref[...].reshape(L, ct).astype(y_ref.dtype)


def _long_conv_latent(z, filt, spare):
    m1, m1i, tf, ti = _two_stage_dft()
    ct = FFT_CT
    n_ct = D // ct
    const = lambda shape: pl.BlockSpec(shape, lambda *_: (0,) * len(shape))
    scratch = [pltpu.VMEM((FFT_H1, FFT_N2, ct), F32), pltpu.VMEM((FFT_R, FFT_N2, ct), F32)]
    hf = pl.pallas_call(
        _filter_spectrum_kernel,
        grid=(n_ct,),
        in_specs=[
            pl.BlockSpec((L, ct), lambda c: (0, c)),
            pl.BlockSpec((L, ct), lambda c: (0, n_ct + c)),
            const(m1.shape), const(tf.shape),
        ],
        out_specs=pl.BlockSpec((FFT_K1, 2 * FFT_N2, ct), lambda c: (0, 0, c)),
        out_shape=jax.ShapeDtypeStruct((FFT_K1, 2 * FFT_N2, D), BF16),
        scratch_shapes=scratch + [pltpu.VMEM((FFT_K1, 2 * FFT_N2, ct), F32)],
        compiler_params=_cparams(("parallel",), 48),
        name="filter_spectrum",
    )(filt, filt, m1, tf)
    return pl.pallas_call(
        _long_conv_kernel,
        grid=(n_ct, B),
        in_specs=[
            pl.BlockSpec((L, ct), lambda c, b: (b, c)),
            pl.BlockSpec((FFT_K1, 2 * FFT_N2, ct), lambda c, b: (0, 0, c)),
            const(m1.shape), const(tf.shape), const(ti.shape), const(m1i.shape),
            pl.BlockSpec(memory_space=pl.ANY),
        ],
        out_specs=pl.BlockSpec((L, ct), lambda c, b: (b, c)),
        out_shape=jax.ShapeDtypeStruct((T, D), BF16),
        input_output_aliases={6: 0},
        scratch_shapes=scratch + [pltpu.VMEM((FFT_R, FFT_N2, ct), F32)],
        compiler_params=_cparams(("parallel", "arbitrary"), 56),
        name="long_conv",
    )(z, hf, m1, tf, ti, m1i, spare)


def _ctx_conv_kernel(z_ref, filt_ref, fwd_ref, inv_ref, _, y_ref):
    half = fwd_ref.shape[0] // 2
    xf = jnp.dot(fwd_ref[...], filt_ref[...], preferred_element_type=F32)
    hr = xf[0:half, 0:D] + xf[0:half, D:2 * D]
    hi = xf[half:, 0:D] - xf[half:, D:2 * D]
    x = jnp.dot(fwd_ref[...], z_ref[...], preferred_element_type=F32)
    xr, xi = x[0:half], x[half:]
    p = jnp.concatenate([xr * hr - xi * hi, xr * hi + xi * hr], axis=0).astype(BF16)
    y_ref[...] = jnp.dot(inv_ref[...], p, preferred_element_type=F32).astype(y_ref.dtype)


def _hyena_long_conv(z, filt_lat, filt_ctx, spare):
    y = _long_conv_latent(z, filt_lat, spare)
    fwd, inv = _direct_dft(CTX)
    full = lambda shape: pl.BlockSpec(shape, lambda b: (0,) * len(shape))
    ctx_rows = pl.BlockSpec((CTX, D), lambda b: (T_LAT // CTX + b, 0))
    return pl.pallas_call(
        _ctx_conv_kernel,
        grid=(B,),
        in_specs=[ctx_rows, full(filt_ctx.shape), full(fwd.shape), full(inv.shape),
                  pl.BlockSpec(memory_space=pl.ANY)],
        out_specs=ctx_rows,
        out_shape=jax.ShapeDtypeStruct((T, D), BF16),
        input_output_aliases={4: 0},
        compiler_params=_cparams(("parallel",), 48),
        name="ctx_conv",
    )(z, filt_ctx, fwd, inv, y)


def _layer_tail_kernel(*refs, hyena):
    h3, y3, refs = refs[0:3], refs[3:6], refs[6:]
    if hyena:
        z3, x03, skip_ref, refs = refs[0:3], refs[3:6], refs[6], refs[7:]
    mod_ref, g_ref, wo_ref, bo_ref, wup_ref, cw_ref, cb_ref, wd_ref, o_ref, xn_ref, gate_ref = refs
    i = pl.program_id(0)
    m = mod_ref[...]
    half = TM // 2
    ext_half = half + HALO

    def ext_rows(trio, s):
        prev_ref, main_ref, next_ref = trio[1], trio[0], trio[2]
        if s == 0:
            return jnp.concatenate([prev_ref[...], main_ref[0:half, :]], axis=0)
        return jnp.concatenate([main_ref[half:, :], next_ref[...]], axis=0)

    for s in range(2):
        mix = ext_rows(y3, s)
        if hyena:
            mix = ((mix.astype(F32) + ext_rows(z3, s).astype(F32) * skip_ref[...])
                   * ext_rows(x03, s).astype(F32)).astype(BF16)
        r = jnp.dot(mix, wo_ref[...], preferred_element_type=F32) + bo_ref[...]
        h_mid = ext_rows(h3, s) + m[:, 2 * D:3 * D] * _rms(r, g_ref[1:2, :])
        xn_ref[s * ext_half:(s + 1) * ext_half, :] = _norm_mod(
            h_mid, g_ref[2:3, :], m[:, 3 * D:4 * D], m[:, 4 * D:5 * D]).astype(BF16)
        o_ref[s * half:(s + 1) * half, :] = h_mid[HALO:, :] if s == 0 else h_mid[0:half, :]

    has_prev, has_next = _seg_masks(i)
    ext = SUB_ROWS + 2 * HALO
    n_sb = TM // SUB_ROWS
    units = [(c0, fc, sb) for c0, fc in FC_CHUNKS for sb in range(n_sb)]

    def up_matmuls(k):
        c0, fc, sb = units[k]
        r0 = sb * SUB_ROWS
        gate_ref[k % 2, :, 0:fc] = jnp.dot(xn_ref[r0:r0 + ext, :], wup_ref[:, c0:c0 + fc],
                                           preferred_element_type=F32)
        return jnp.dot(xn_ref[HALO + r0:HALO + r0 + SUB_ROWS, :],
                       wup_ref[:, D_FF + c0:D_FF + c0 + fc], preferred_element_type=F32)

    acc = [None] * n_sb
    up_next = up_matmuls(0)
    for k, (c0, fc, sb) in enumerate(units):
        up = up_next
        if k + 1 < len(units):
            up_next = up_matmuls(k + 1)
        cols = slice(c0, c0 + fc)
        rows = slice(sb * SUB_ROWS, (sb + 1) * SUB_ROWS)
        conv = _conv3(gate_ref.at[k % 2, :, 0:fc], cw_ref[:, cols], cb_ref[:, cols], has_prev[rows], has_next[rows],
                      SUB_ROWS)
        part = jnp.dot((_silu(conv) * up).astype(BF16), wd_ref[cols, :], preferred_element_type=F32)
        acc[sb] = part if c0 == 0 else acc[sb] + part
    for sb in range(n_sb):
        rows = slice(sb * SUB_ROWS, (sb + 1) * SUB_ROWS)
        o_ref[rows, :] = o_ref[rows, :] + m[:, 5 * D:6 * D] * _rms(acc[sb], g_ref[3:4, :])


def _layer_tail(h, y, mods_l, g, w_out, b_out, layer, w_up, cw, cb, w_down, n_tiles, hyena_args=None):
    resident = lambda shape: pl.BlockSpec(shape, lambda i: (0,) * len(shape), pipeline_mode=pl.Buffered(1))
    stacked = lambda shape: pl.BlockSpec((None,) + shape, lambda i: (layer,) + (0,) * len(shape),
                                         pipeline_mode=pl.Buffered(1))
    hyena = hyena_args is not None
    in_specs = _row_specs(h.shape[0]) + _row_specs(y.shape[0])
    args = [h, h, h, y, y, y]
    if hyena:
        z, x0, skip = hyena_args
        in_specs += _row_specs(z.shape[0]) + _row_specs(x0.shape[0]) + [resident((1, D))]
        args += [z, z, z, x0, x0, x0, skip]
    in_specs += [
        pl.BlockSpec((None, 1, N_MOD * D), lambda i: (_mod_row(i), 0, 0)),
        pl.BlockSpec((4, D), lambda i: (0, 0)),
        resident((D, D)), resident((1, D)),
        stacked((D, 2 * D_FF)), stacked((3, D_FF)), stacked((1, D_FF)), stacked((D_FF, D)),
    ]
    args += [mods_l, g, w_out, b_out, w_up, cw, cb.reshape(DEPTH, 1, D_FF), w_down]
    return pl.pallas_call(
        functools.partial(_layer_tail_kernel, hyena=hyena),
        grid=(n_tiles,),
        in_specs=in_specs,
        out_specs=pl.BlockSpec((TM, D), lambda i: (i, 0)),
        out_shape=jax.ShapeDtypeStruct((n_tiles * TM, D), F32),
        scratch_shapes=[
            pltpu.VMEM((TM + 2 * HALO, D), BF16),
            pltpu.VMEM((2, SUB_ROWS + 2 * HALO, FC), F32),
        ],
        compiler_params=_cparams(("parallel",), 56),
        name="layer_tail",
    )(*args)


def _dup_heads(w):
    k0, k1 = w[..., 0:C_DH], w[..., C_DH:2 * C_DH]
    return jnp.concatenate([k0, k0, k1, k1], axis=-1)


def kernel(x, c, ctx, c_ctx, mod_w, mod_b, norm_g, ffn_w_up, ffn_conv_w, ffn_conv_b, ffn_w_down, a_w_qkv, a_q_gain, a_k_gain, a_w_out, b_w_in, b_b_in, b_conv_w, b_conv_b, b_f_w1, b_f_b1, b_f_w2, b_f_b2, b_f_w3, b_f_b3, b_f_w4, b_skip, b_w_out, b_b_out, c_w_qkv, c_b_qkv, c_sink, c_w_out, c_b_out):
    h = (x.reshape(T_LAT, D), ctx.reshape(T_CTX, D))
    cs = jnp.concatenate([c, c_ctx[None], jnp.zeros((8 - B - 1, D), F32)], axis=0)
    mods = _mods(cs, mod_w, mod_b).reshape(DEPTH, 8, 1, N_MOD * D)
    cos_a, sin_a = _rope_tables(A_DH)
    cos_c, sin_c = _rope_tables(C_DH)
    zeros_d = jnp.zeros((1, D), F32)
    w_up_bf16, w_down_bf16 = ffn_w_up.astype(BF16), ffn_w_down.astype(BF16)
    y = None

    for i in range(DEPTH):
        last = i == DEPTH - 1
        kind, j = i % 3, i // 3
        g = norm_g[i]
        mods_l = mods[i]
        n_tiles = N_LAT_TILES if last else N_TILES
        if kind == 0:
            qkv = _proj_attn(h, mods_l, g, a_w_qkv[j].astype(BF16), jnp.zeros((1, QKV_COLS), F32),
                             a_q_gain[j].reshape(1, A_DH), a_k_gain[j].reshape(1, A_DH), cos_a, sin_a, "a")
            if isinstance(h, tuple):
                qkv, h = qkv
            y = _attn_a(qkv, with_ctx=not last)
            w_out, b_out, hyena_args = a_w_out[j], zeros_d, None
        elif kind == 1:
            x0, z = _proj_b(h, mods_l, g, b_w_in[j].astype(BF16), b_b_in[j].reshape(1, 3 * D),
                            b_conv_w[j], b_conv_b[j].reshape(1, 3 * D))
            fw = (b_f_w1[j], b_f_b1[j], b_f_w2[j], b_f_b2[j], b_f_w3[j], b_f_b3[j], b_f_w4[j])
            spare = y if y is not None and y.shape == (T, D) else jnp.zeros((T, D), BF16)
            y = _hyena_long_conv(z, _hyena_filter(L, *fw), _hyena_filter(CTX, *fw), spare)
            w_out, b_out = b_w_out[j], b_b_out[j].reshape(1, D)
            hyena_args = (z, x0, b_skip[j].reshape(1, D))
        else:
            qc = C_HEADS * C_DH
            kc = qc + C_KV * C_DH
            w, bias = c_w_qkv[j], c_b_qkv[j].reshape(1, -1)
            w = jnp.concatenate([w[:, :qc], _dup_heads(w[:, qc:kc]), _dup_heads(w[:, kc:])], axis=1)
            bias = jnp.concatenate([bias[:, :qc], _dup_heads(bias[:, qc:kc]), _dup_heads(bias[:, kc:])], axis=1)
            ones = jnp.ones((1, 128), F32)
            qkv = _proj_attn(h, mods_l, g, w.astype(BF16), bias, ones, ones, cos_c, sin_c, "c")
            sink_rows = jnp.broadcast_to((c_sink[j].astype(F32) * LOG2_E)[:, None, None],
                                         (C_HEADS, 8, 128)).reshape(C_KV, (C_HEADS // C_KV) * 8, 128)
            y = _attn_c(qkv, sink_rows)
            w_out, b_out, hyena_args = c_w_out[j], c_b_out[j].reshape(1, D), None
        h = _layer_tail(h, y, mods_l, g, w_out.astype(BF16), b_out, i, w_up_bf16, ffn_conv_w, ffn_conv_b,
                        w_down_bf16, n_tiles, hyena_args)
    return h.reshape(B, L, D)
```

```python
import functools
import math

import numpy as np
import jax
import jax.numpy as jnp
from jax import lax
from jax.experimental import pallas as pl
from jax.experimental.pallas import tpu as pltpu

F32 = jnp.float32
BF16 = jnp.bfloat16

D = 1024
B = 4
L = 4096
CTX = 256
DEPTH = 4
N_MOD = 6
EPS = 1e-6
NEG_INF = -1e30
LOG2_E = math.log2(math.e)
GRID_W = 64
ROPE_THETA = 10000.0

A_HEADS, A_KV, A_DH = 8, 2, 128
C_HEADS, C_KV, C_DH = 16, 2, 64
QKV_COLS = 1536
PROJ_GROUP = 4
SUB_ROWS = 256
PROJ_ROWS = 256
D_FF = 2816
FILTER_BANDS = 16
FILTER_EMB = 1 + 2 * FILTER_BANDS
FILTER_WIDTH = 64
FILTER_SIN_FREQ = 1.0
DECAY_TARGET = 1e-2
DECAY_FAST_PCT = 0.3
DECAY_SLOW_PCT = 1.5

T_LAT = B * L
T_CTX = B * CTX
T = T_LAT + T_CTX

TM = 512
HALO = 16
N_LAT_TILES = T_LAT // TM
N_TILES = T // TM
MXU_W = 256
FC_CHUNKS = ((0, 6 * MXU_W), (6 * MXU_W, 5 * MXU_W))
FC = max(w for _, w in FC_CHUNKS)
WINDOW = 128
TQ = 256
TQA = 256
NQ_LAT = L // TQ
NQ_CTX = CTX // TQ

FFT_N = 2 * L
FFT_N1 = 64
FFT_N2 = FFT_N // FFT_N1
FFT_H1 = FFT_N1 // 2
FFT_K1 = FFT_H1 + 1
FFT_R = 72
FFT_CT = 256
FFT_SUB = 8
FFT_UNROLL = 33
FFT_UNROLL1 = 8

VMEM_MB = 1024 * 1024


def _cparams(sem, vmem_mb):
    return pltpu.CompilerParams(dimension_semantics=sem, vmem_limit_bytes=vmem_mb * VMEM_MB)


def _rope_tables(head_dim):
    rows = L // GRID_W
    row = np.repeat(np.arange(rows), GRID_W).astype(np.float64)
    col = np.tile(np.arange(GRID_W), rows).astype(np.float64)
    axis_dim = head_dim // 2
    inv = np.power(ROPE_THETA, -np.arange(0, axis_dim, 2, dtype=np.float64) / axis_dim)
    ang = np.concatenate([row[:, None] * inv[None], col[:, None] * inv[None]], axis=-1)
    cos, sin = np.cos(ang), np.sin(ang)
    reps = 128 // head_dim
    cos_t = np.tile(np.concatenate([cos, cos], axis=-1), (1, reps))
    sin_t = np.tile(np.concatenate([-sin, sin], axis=-1), (1, reps))
    cos_t = np.concatenate([cos_t, np.ones((TM, 128))], axis=0)
    sin_t = np.concatenate([sin_t, np.zeros((TM, 128))], axis=0)
    return jnp.asarray(cos_t, F32), jnp.asarray(sin_t, F32)


def _two_stage_dft():
    n1s, n2s, h1, k1n, n = FFT_N1, FFT_N2, FFT_H1, FFT_K1, FFT_N
    n1 = np.arange(h1)
    k1 = np.arange(k1n)
    ang = 2 * np.pi * np.outer(k1, n1) / n1s
    m1 = np.zeros((FFT_R, h1))
    m1[0:2 * k1n:2] = np.cos(ang)
    m1[1:2 * k1n:2] = -np.sin(ang)
    c = np.where((k1 == 0) | (k1 == h1), 1.0, 2.0)
    m1i = np.zeros((h1, FFT_R))
    m1i[:, 0:2 * k1n:2] = (c[None] / n) * np.cos(ang.T)
    m1i[:, 1:2 * k1n:2] = -(c[None] / n) * np.sin(ang.T)
    n2 = np.arange(n2s)
    k2 = np.arange(n2s)
    tf = np.zeros((k1n, 2 * n2s, 2 * n2s))
    ti = np.zeros((k1n, 2 * n2s, 2 * n2s))
    for a in range(k1n):
        th = 2 * np.pi * np.outer(a + n1s * k2, n2) / n
        er, ei = np.cos(th), -np.sin(th)
        tf[a] = np.block([[er, -ei], [ei, er]])
        er, ei = np.cos(th.T), np.sin(th.T)
        ti[a] = np.block([[er, -ei], [ei, er]])
    eye = np.eye(FFT_SUB)
    return tuple(jnp.asarray(a, F32).astype(BF16) for a in (np.kron(m1, eye), np.kron(m1i, eye), tf, ti))


def _direct_dft(seq_len):
    n = 2 * seq_len
    th = 2 * np.pi * np.outer(np.arange(n), np.arange(seq_len)) / n
    fwd = np.concatenate([np.cos(th), -np.sin(th)], axis=0)
    inv = np.concatenate([np.cos(th.T), -np.sin(th.T)], axis=1) / n
    return jnp.asarray(fwd, F32).astype(BF16), jnp.asarray(inv, F32).astype(BF16)


def _filter_features(seq_len):
    t = np.linspace(0.0, 1.0, seq_len)[:, None]
    omega = (2.0 * math.pi / seq_len) * np.arange(seq_len, dtype=np.float64)
    bands = np.linspace(1e-4, FILTER_BANDS - 1, FILTER_BANDS)
    ang = omega[:, None] * bands[None, :]
    z = np.concatenate([t, np.cos(ang), -np.sin(ang)], axis=-1)
    z = np.pad(z, ((0, 0), (0, FILTER_WIDTH - FILTER_EMB)))
    rates = np.abs(np.linspace(math.log(DECAY_TARGET) / DECAY_FAST_PCT,
                               math.log(DECAY_TARGET) / DECAY_SLOW_PCT, D))[None]
    return jnp.asarray(z, F32), jnp.asarray(t, F32), jnp.asarray(rates, F32)


def _rms(x, g):
    return x * lax.rsqrt(jnp.mean(x * x, axis=-1, keepdims=True) + EPS) * g


def _norm_mod(h, g, shift, scale):
    return _rms(h, g) * (1.0 + scale) + shift


def _silu(x):
    return x * (1.0 / (1.0 + jnp.exp(-x)))


def _seg_masks(i):
    rows = i * TM + lax.broadcasted_iota(jnp.int32, (TM, 1), 0)
    seg = jnp.where(i < N_LAT_TILES, L, CTX)
    pos = rows & (seg - 1)
    return (pos != 0).astype(F32), (pos != seg - 1).astype(F32)


def _fill_xn(xn_ref, h_ref, hp_ref, hn_ref, g, shift, scale):
    xn_ref[0:HALO, :] = _norm_mod(hp_ref[...], g, shift, scale).astype(BF16)
    xn_ref[HALO:HALO + TM, :] = _norm_mod(h_ref[...], g, shift, scale).astype(BF16)
    xn_ref[HALO + TM:, :] = _norm_mod(hn_ref[...], g, shift, scale).astype(BF16)


def _conv3(u_ref, cw, cb, has_prev, has_next, n_rows):
    u = u_ref[...]
    total = n_rows + 2 * HALO
    u_prev = pltpu.roll(u, 1, 0)
    u_next = pltpu.roll(u, total - 1, 0)

    def piece(r0, r1, masked):
        prev = u_prev[HALO + r0:HALO + r1]
        nxt = u_next[HALO + r0:HALO + r1]
        if masked:
            prev = prev * has_prev[r0:r1]
            nxt = nxt * has_next[r0:r1]
        return cb + cw[0:1] * prev + cw[1:2] * u[HALO + r0:HALO + r1] + cw[2:3] * nxt

    pieces = []
    for s in range(0, n_rows, CTX):
        pieces += [piece(s, s + 8, True), piece(s + 8, s + CTX - 8, False), piece(s + CTX - 8, s + CTX, True)]
    return jnp.concatenate(pieces, axis=0)


def _mod_row(i):
    return jnp.where(i < N_LAT_TILES, i // (L // TM), B)


def _row_specs(in_rows):
    last = in_rows // HALO - 1
    per = TM // HALO
    return [
        pl.BlockSpec((TM, D), lambda i, *_: (i, 0)),
        pl.BlockSpec((HALO, D), lambda i, *_: (jnp.maximum(i * per - 1, 0), 0)),
        pl.BlockSpec((HALO, D), lambda i, *_: (jnp.minimum((i + 1) * per, last), 0)),
    ]


def _mods_kernel(s_ref, w_ref, b_ref, o_ref):
    s = _silu(s_ref[...])
    o_ref[...] = jnp.dot(s, w_ref[...], preferred_element_type=F32,
                         precision=lax.Precision.HIGHEST) + b_ref[...]


def _mods(cs, mod_w, mod_b):
    tn = 1536
    return pl.pallas_call(
        _mods_kernel,
        grid=(DEPTH, N_MOD * D // tn),
        in_specs=[
            pl.BlockSpec((8, D), lambda l, n: (0, 0)),
            pl.BlockSpec((None, D, tn), lambda l, n: (l, 0, n)),
            pl.BlockSpec((None, 1, tn), lambda l, n: (l, 0, n)),
        ],
        out_specs=pl.BlockSpec((None, 8, tn), lambda l, n: (l, 0, n)),
        out_shape=jax.ShapeDtypeStruct((DEPTH, 8, N_MOD * D), F32),
        compiler_params=_cparams(("parallel", "parallel"), 40),
        name="mods",
    )(cs, mod_w, mod_b.reshape(DEPTH, 1, N_MOD * D))


def _proj_attn_kernel(*refs, kind, assemble):
    if assemble:
        lat_ref, ctx_ref, refs, h_out_ref = refs[0], refs[1], refs[2:-1], refs[-1]
        is_lat = pl.program_id(0) < N_LAT_TILES
    else:
        h_ref, refs = refs[0], refs[1:]
    mod_ref, g_ref, w_ref, b_ref, qg_ref, kg_ref, cos_ref, sin_ref, o_ref = refs
    m = mod_ref[...]
    n_q = 8
    n_rope = 10
    q_scale = (A_DH if kind == "a" else C_DH) ** -0.5 * LOG2_E
    if kind == "c":
        lane = lax.broadcasted_iota(jnp.int32, (PROJ_ROWS, 128), 1)
        first_half = (lane & (C_DH - 1)) < C_DH // 2
    for r0 in range(0, TM, PROJ_ROWS):
        rows = slice(r0, r0 + PROJ_ROWS)
        if assemble:
            h = jnp.where(is_lat, lat_ref[rows, :], ctx_ref[rows, :])
            h_out_ref[rows, :] = h
        else:
            h = h_ref[rows, :]
        xn = _norm_mod(h, g_ref[0:1, :], m[:, 0:D], m[:, D:2 * D]).astype(BF16)
        cos, sin = cos_ref[rows, :], sin_ref[rows, :]
        cos_q, sin_q = cos * q_scale, sin * q_scale
        for j in range(QKV_COLS // 128):
            if j % PROJ_GROUP == 0:
                wide = slice(j * 128, (j + PROJ_GROUP) * 128)
                y_wide = jnp.dot(xn, w_ref[:, wide], preferred_element_type=F32) + b_ref[:, wide]
            y = y_wide[:, (j % PROJ_GROUP) * 128:(j % PROJ_GROUP + 1) * 128]
            if j < n_rope:
                if kind == "a":
                    y = _rms(y, qg_ref[...] if j < n_q else kg_ref[...])
                    rot = pltpu.roll(y, 64, 1)
                else:
                    rot = jnp.where(first_half, pltpu.roll(y, 128 - C_DH // 2, 1), pltpu.roll(y, C_DH // 2, 1))
                y = y * cos_q + rot * sin_q if j < n_q else y * cos + rot * sin
            o_ref[rows, j * 128:(j + 1) * 128] = y.astype(BF16)


def _proj_attn(h, mods_l, g, w, b, q_gain, k_gain, cos_t, sin_t, kind):
    tpb = L // TM
    rope_idx = lambda i: (jnp.where(i < N_LAT_TILES, i % tpb, tpb), 0)
    assemble = isinstance(h, tuple)
    tile = pl.BlockSpec((TM, D), lambda i: (i, 0))
    if assemble:
        h_specs = [pl.BlockSpec((TM, D), lambda i: (jnp.minimum(i, N_LAT_TILES - 1), 0)),
                   pl.BlockSpec((TM, D), lambda i: (jnp.maximum(i - N_LAT_TILES, 0), 0))]
        h_args = list(h)
    else:
        h_specs, h_args = [tile], [h]
    qkv_spec = pl.BlockSpec((TM, QKV_COLS), lambda i: (i, 0))
    qkv_shape = jax.ShapeDtypeStruct((T, QKV_COLS), BF16)
    return pl.pallas_call(
        functools.partial(_proj_attn_kernel, kind=kind, assemble=assemble),
        grid=(N_TILES,),
        in_specs=h_specs + [
            pl.BlockSpec((None, 1, N_MOD * D), lambda i: (_mod_row(i), 0, 0)),
            pl.BlockSpec((4, D), lambda i: (0, 0)),
            pl.BlockSpec((D, QKV_COLS), lambda i: (0, 0)),
            pl.BlockSpec((1, QKV_COLS), lambda i: (0, 0)),
            pl.BlockSpec((1, 128), lambda i: (0, 0)),
            pl.BlockSpec((1, 128), lambda i: (0, 0)),
            pl.BlockSpec((TM, 128), rope_idx),
            pl.BlockSpec((TM, 128), rope_idx),
        ],
        out_specs=[qkv_spec, tile] if assemble else qkv_spec,
        out_shape=[qkv_shape, jax.ShapeDtypeStruct((T, D), F32)] if assemble else qkv_shape,
        compiler_params=_cparams(("parallel",), 40),
        name="proj_" + kind,
    )(*h_args, mods_l, g, w, b, q_gain, k_gain, cos_t, sin_t)


def _qk(q, k):
    return lax.dot_general(q, k, (((1,), (1,)), ((), ())), preferred_element_type=F32)


def _ones_ext(v):
    return jnp.concatenate([v, jnp.ones(v.shape, BF16)], axis=1)


def _attn_a_kernel(q_ref, kl_ref, vl_ref, kc_ref, vc_ref, o_ref, vext_ref, *, with_ctx):
    @pl.when(pl.program_id(2) == 0)
    def _():
        vext_ref[0:CTX, 0:A_DH] = vc_ref[...]
        vext_ref[CTX:, 0:A_DH] = vl_ref[...]
        vext_ref[:, A_DH:] = jnp.ones((CTX + L, A_DH), BF16)

    ctx_piece = (kc_ref, vext_ref.at[0:CTX])
    lat_piece = (kl_ref, vext_ref.at[CTX:])
    if with_ctx:
        is_ctx_step = pl.program_id(2) == L // TQA

        @pl.when(is_ctx_step)
        def _():
            _attn_a_body(q_ref, [ctx_piece], o_ref)

        @pl.when(jnp.logical_not(is_ctx_step))
        def _():
            _attn_a_body(q_ref, [ctx_piece, lat_piece], o_ref)
    else:
        _attn_a_body(q_ref, [ctx_piece, lat_piece], o_ref)


def _attn_a_body(q_ref, pieces, o_ref):
    g = A_HEADS // A_KV
    scores = [jnp.concatenate([_qk(q_ref[:, i * A_DH:(i + 1) * A_DH], k_ref[...]) for k_ref, _ in pieces], axis=1)
              for i in range(g)]
    for i in range(g):
        s = scores[i]
        m = jnp.broadcast_to(jnp.max(s, axis=-1, keepdims=True), (TQA, 128))
        p = jnp.exp2(s - jnp.tile(m, (1, s.shape[1] // 128))).astype(BF16)
        pv, off = None, 0
        for k_ref, vext in pieces:
            n = k_ref.shape[0]
            part = jnp.dot(p[:, off:off + n], vext[...], preferred_element_type=F32)
            pv = part if pv is None else pv + part
            off += n
        o_ref[:, i * A_DH:(i + 1) * A_DH] = (pv[:, 0:A_DH] / pv[:, A_DH:]).astype(BF16)


def _attn_a(qkv, with_ctx):
    assert CTX == TQA
    nq = L // TQA
    gcols = (A_HEADS // A_KV) * A_DH // 128
    q_spec = pl.BlockSpec((TQA, gcols * 128),
                          lambda b, kv, j: (jnp.where(j < nq, b * nq + j, T_LAT // TQA + b), kv))
    return pl.pallas_call(
        functools.partial(_attn_a_kernel, with_ctx=with_ctx),
        grid=(B, A_KV, nq + with_ctx),
        in_specs=[
            q_spec,
            pl.BlockSpec((L, A_DH), lambda b, kv, j: (b, A_HEADS + kv)),
            pl.BlockSpec((L, A_DH), lambda b, kv, j: (b, A_HEADS + A_KV + kv)),
            pl.BlockSpec((CTX, A_DH), lambda b, kv, j: (T_LAT // CTX + b, A_HEADS + kv)),
            pl.BlockSpec((CTX, A_DH), lambda b, kv, j: (T_LAT // CTX + b, A_HEADS + A_KV + kv)),
        ],
        out_specs=q_spec,
        out_shape=jax.ShapeDtypeStruct((T if with_ctx else T_LAT, D), BF16),
        scratch_shapes=[pltpu.VMEM((CTX + L, 2 * A_DH), BF16)],
        compiler_params=_cparams(("parallel", "parallel", "arbitrary"), 48),
        name="attn_a",
    )(qkv, qkv, qkv, qkv, qkv)


def _attn_c_kernel(q_ref, kl_ref, vl_ref, kx_ref, vx_ref, sink_ref, o_ref):
    j = pl.program_id(2)
    g = C_HEADS // C_KV
    lane = lax.broadcasted_iota(jnp.int32, (TQ, 128), 1)
    lo = lane < C_DH
    zero = jnp.zeros((TQ, 128), BF16)
    is_lat = j < NQ_LAT
    q0 = j * TQ
    win = TQ + 2 * WINDOW
    s0 = pl.multiple_of(jnp.clip(q0 - WINDOW, 0, L - win), 128)
    k_all = jnp.concatenate([kx_ref[...], kl_ref[pl.ds(s0, win), :]], axis=0)
    v_ext = _ones_ext(jnp.concatenate([vx_ref[...], vl_ref[pl.ds(s0, win), :]], axis=0))
    rel = (lax.broadcasted_iota(jnp.int32, (TQ, win), 1) - lax.broadcasted_iota(jnp.int32, (TQ, win), 0)
           + jnp.where(is_lat, s0 - q0, 4 * win))
    bias = jnp.concatenate([jnp.zeros((TQ, CTX), F32), jnp.where(jnp.abs(rel) <= WINDOW, 0.0, NEG_INF)], axis=1)
    heads = []
    for p in range(g // 2):
        qb = q_ref[:, p * 128:(p + 1) * 128]
        heads += [jnp.where(lo, qb, zero), jnp.where(lo, zero, qb)]
    scores = [_qk(qh, k_all) + bias for qh in heads]
    outs = []
    for h in range(g):
        s = scores[h]
        sink = jnp.tile(sink_ref[h * 8:(h + 1) * 8, :], (TQ // 8, 1))
        m_part = sink
        for c in range(s.shape[1] // 128):
            m_part = jnp.maximum(m_part, s[:, c * 128:(c + 1) * 128])
        m = jnp.broadcast_to(jnp.max(m_part, axis=-1, keepdims=True), (TQ, 128))
        p = jnp.exp2(s - jnp.tile(m, (1, s.shape[1] // 128))).astype(BF16)
        pv = jnp.dot(p, v_ext, preferred_element_type=F32)
        outs.append(pv[:, 0:128] / (pv[:, 128:] + jnp.exp2(sink - m)))
    for p in range(g // 2):
        o_ref[:, p * 128:(p + 1) * 128] = jnp.where(lo, outs[2 * p], outs[2 * p + 1]).astype(BF16)


def _attn_c(qkv, sink_rows):
    nq = NQ_LAT + NQ_CTX
    qcols = C_HEADS * C_DH // C_KV
    kcol = C_HEADS * C_DH // 128
    vcol = kcol + C_KV
    n_lat_blocks = T_LAT // TQ

    def q_idx(b, kv, j):
        return (jnp.where(j < NQ_LAT, b * NQ_LAT + j, n_lat_blocks + b * NQ_CTX + (j - NQ_LAT)), kv)

    lat = lambda col: pl.BlockSpec((L, 128), lambda b, kv, j: (b, col + kv))
    ctx = lambda col: pl.BlockSpec((CTX, 128), lambda b, kv, j: (T_LAT // CTX + b, col + kv))
    return pl.pallas_call(
        _attn_c_kernel,
        grid=(B, C_KV, nq),
        in_specs=[
            pl.BlockSpec((TQ, qcols), q_idx),
            lat(kcol), lat(vcol), ctx(kcol), ctx(vcol),
            pl.BlockSpec((None, (C_HEADS // C_KV) * 8, 128), lambda b, kv, j: (kv, 0, 0)),
        ],
        out_specs=pl.BlockSpec((TQ, qcols), q_idx),
        out_shape=jax.ShapeDtypeStruct((T, D), BF16),
        compiler_params=_cparams(("parallel", "parallel", "arbitrary"), 48),
        name="attn_c",
    )(qkv, qkv, qkv, qkv, qkv, sink_rows)


def _proj_b_kernel(h_ref, hp_ref, hn_ref, mod_ref, g_ref, w_ref, b_ref, cw_ref, cb_ref,
                   x0_ref, z_ref, xn_ref, u_ref):
    i = pl.program_id(0)
    m = mod_ref[...]
    _fill_xn(xn_ref, h_ref, hp_ref, hn_ref, g_ref[0:1, :], m[:, 0:D], m[:, D:2 * D])
    has_prev, has_next = _seg_masks(i)

    def matmul(k):
        u_ref[k % 2] = (jnp.dot(xn_ref[...], w_ref[:, k * D:(k + 1) * D], preferred_element_type=F32)
                        + b_ref[:, k * D:(k + 1) * D])

    matmul(0)
    x1 = None
    for k in range(3):
        if k + 1 < 3:
            matmul(k + 1)
        cols = slice(k * D, (k + 1) * D)
        conv = _conv3(u_ref.at[k % 2], cw_ref[:, cols], cb_ref[:, cols], has_prev, has_next, TM)
        if k == 0:
            x0_ref[...] = conv.astype(BF16)
        elif k == 1:
            x1 = conv
        else:
            z_ref[...] = (conv * x1).astype(BF16)


def _proj_b(h, mods_l, g, w, b, cw, cb):
    full = lambda shape: pl.BlockSpec(shape, lambda i: (0,) * len(shape))
    return pl.pallas_call(
        _proj_b_kernel,
        grid=(N_TILES,),
        in_specs=_row_specs(T) + [
            pl.BlockSpec((None, 1, N_MOD * D), lambda i: (_mod_row(i), 0, 0)),
            full((4, D)), full((D, 3 * D)), full((1, 3 * D)), full((3, 3 * D)), full((1, 3 * D)),
        ],
        out_specs=[pl.BlockSpec((TM, D), lambda i: (i, 0))] * 2,
        out_shape=[jax.ShapeDtypeStruct((T, D), BF16)] * 2,
        scratch_shapes=[pltpu.VMEM((TM + 2 * HALO, D), BF16), pltpu.VMEM((2, TM + 2 * HALO, D), F32)],
        compiler_params=_cparams(("parallel",), 48),
        name="proj_b",
    )(h, h, h, mods_l, g, w, b, cw, cb)


def _filter_kernel(zf_ref, t_ref, w1_ref, b1_ref, w2_ref, b2_ref, w3_ref, b3_ref, w4_ref, rates_ref, o_ref):
    i = pl.program_id(0)
    tm = zf_ref.shape[0]
    dot = functools.partial(jnp.dot, preferred_element_type=F32, precision=lax.Precision.HIGHEST)
    f = jnp.sin(FILTER_SIN_FREQ * (dot(zf_ref[...], w1_ref[...]) + b1_ref[...]))
    f = jnp.sin(FILTER_SIN_FREQ * (dot(f, w2_ref[...]) + b2_ref[...]))
    f = jnp.sin(FILTER_SIN_FREQ * (dot(f, w3_ref[...]) + b3_ref[...]))
    f = jnp.dot(f.astype(BF16), w4_ref[...].astype(BF16), preferred_element_type=F32)
    decay = jnp.exp(-t_ref[...] * rates_ref[...])
    row = i * tm + lax.broadcasted_iota(jnp.int32, (tm, 1), 0)
    o_ref[:, 0:D] = (f[:, 0:D] * decay).astype(BF16)
    o_ref[:, D:2 * D] = (f[:, D:2 * D] * decay * (row != 0).astype(F32)).astype(BF16)


def _hyena_filter(seq_len, w1, b1, w2, b2, w3, b3, w4):
    zf, t, rates = _filter_features(seq_len)
    tm = min(seq_len, 512)
    w1p = jnp.pad(w1, ((0, FILTER_WIDTH - FILTER_EMB), (0, 0)))
    full = lambda shape: pl.BlockSpec(shape, lambda i: (0,) * len(shape))
    fw = FILTER_WIDTH
    return pl.pallas_call(
        _filter_kernel,
        grid=(seq_len // tm,),
        in_specs=[
            pl.BlockSpec((tm, fw), lambda i: (i, 0)),
            pl.BlockSpec((tm, 1), lambda i: (i, 0)),
            full((fw, fw)), full((1, fw)), full((fw, fw)), full((1, fw)), full((fw, fw)), full((1, fw)),
            full((fw, 2 * D)), full((1, D)),
        ],
        out_specs=pl.BlockSpec((tm, 2 * D), lambda i: (i, 0)),
        out_shape=jax.ShapeDtypeStruct((seq_len, 2 * D), BF16),
        compiler_params=_cparams(("parallel",), 40),
        name="hyena_filter",
    )(zf, t, w1p, b1.reshape(1, fw), w2, b2.reshape(1, fw), w3, b3.reshape(1, fw), w4, rates)


def _dft_forward(x_ref, xs_ref, a_ref, m1_ref, tf_ref, emit):
    ct = x_ref.shape[1]
    xs_ref[...] = x_ref[...].astype(F32).reshape(FFT_H1, FFT_N2, ct)

    def stage1(g, carry):
        rows = pl.ds(pl.multiple_of(g * FFT_SUB, FFT_SUB), FFT_SUB)
        x = xs_ref[:, rows, :].reshape(FFT_H1 * FFT_SUB, ct).astype(BF16)
        a = jnp.dot(m1_ref[...], x, preferred_element_type=F32)
        a_ref[:, rows, :] = a.reshape(FFT_R, FFT_SUB, ct)
        return carry

    lax.fori_loop(0, FFT_N2 // FFT_SUB, stage1, 0, unroll=FFT_UNROLL1)

    def stage2(k1, carry):
        d = a_ref[pl.ds(2 * k1, 2)].reshape(2 * FFT_N2, ct).astype(BF16)
        emit(k1, jnp.dot(tf_ref[k1], d, preferred_element_type=F32))
        return carry

    lax.fori_loop(0, FFT_K1, stage2, 0, unroll=FFT_UNROLL)


def _filter_spectrum_kernel(hf_ref, hb_ref, m1_ref, tf_ref, o_ref, xs_ref, a_ref, f_ref):
    half = FFT_N2

    def keep(k1, x):
        f_ref[k1] = x

    _dft_forward(hf_ref, xs_ref, a_ref, m1_ref, tf_ref, keep)

    def combine(k1, x):
        f = f_ref[k1]
        o_ref[k1, 0:half, :] = (f[0:half] + x[0:half]).astype(BF16)
        o_ref[k1, half:, :] = (f[half:] - x[half:]).astype(BF16)

    _dft_forward(hb_ref, xs_ref, a_ref, m1_ref, tf_ref, combine)


def _long_conv_kernel(z_ref, hf_ref, m1_ref, tf_ref, ti_ref, m1i_ref, _, y_ref, xs_ref, a_ref, g_ref):
    half = FFT_N2
    ct = z_ref.shape[1]
    g_ref[2 * FFT_K1:] = jnp.zeros((FFT_R - 2 * FFT_K1, half, ct), F32)

    def product_and_invert(k1, x):
        hr = hf_ref[k1, 0:half, :].astype(F32)
        hi = hf_ref[k1, half:, :].astype(F32)
        xr, xi = x[0:half], x[half:]
        p = jnp.concatenate([xr * hr - xi * hi, xr * hi + xi * hr], axis=0).astype(BF16)
        g_ref[pl.ds(2 * k1, 2)] = jnp.dot(ti_ref[k1], p, preferred_element_type=F32).reshape(2, half, ct)

    _dft_forward(z_ref, xs_ref, a_ref, m1_ref, tf_ref, product_and_invert)

    def inverse_stage1(g, carry):
        rows = pl.ds(pl.multiple_of(g * FFT_SUB, FFT_SUB), FFT_SUB)
        x = g_ref[:, rows, :].reshape(FFT_R * FFT_SUB, ct).astype(BF16)
        y = jnp.dot(m1i_ref[...], x, preferred_element_type=F32)
        xs_ref[:, rows, :] = y.reshape(FFT_H1, FFT_SUB, ct)
        return carry

    lax.fori_loop(0, FFT_N2 // FFT_SUB, inverse_stage1, 0, unroll=FFT_UNROLL1)
    y_ref[...] = xs_ref[...].reshape(L, ct).astype(y_ref.dtype)


def _long_conv_latent(z, filt, spare):
    m1, m1i, tf, ti = _two_stage_dft()
    ct = FFT_CT
    n_ct = D // ct
    const = lambda shape: pl.BlockSpec(shape, lambda *_: (0,) * len(shape))
    scratch = [pltpu.VMEM((FFT_H1, FFT_N2, ct), F32), pltpu.VMEM((FFT_R, FFT_N2, ct), F32)]
    hf = pl.pallas_call(
        _filter_spectrum_kernel,
        grid=(n_ct,),
        in_specs=[
            pl.BlockSpec((L, ct), lambda c: (0, c)),
            pl.BlockSpec((L, ct), lambda c: (0, n_ct + c)),
            const(m1.shape), const(tf.shape),
        ],
        out_specs=pl.BlockSpec((FFT_K1, 2 * FFT_N2, ct), lambda c: (0, 0, c)),
        out_shape=jax.ShapeDtypeStruct((FFT_K1, 2 * FFT_N2, D), BF16),
        scratch_shapes=scratch + [pltpu.VMEM((FFT_K1, 2 * FFT_N2, ct), F32)],
        compiler_params=_cparams(("parallel",), 48),
        name="filter_spectrum",
    )(filt, filt, m1, tf)
    return pl.pallas_call(
        _long_conv_kernel,
        grid=(n_ct, B),
        in_specs=[
            pl.BlockSpec((L, ct), lambda c, b: (b, c)),
            pl.BlockSpec((FFT_K1, 2 * FFT_N2, ct), lambda c, b: (0, 0, c)),
            const(m1.shape), const(tf.shape), const(ti.shape), const(m1i.shape),
            pl.BlockSpec(memory_space=pl.ANY),
        ],
        out_specs=pl.BlockSpec((L, ct), lambda c, b: (b, c)),
        out_shape=jax.ShapeDtypeStruct((T, D), BF16),
        input_output_aliases={6: 0},
        scratch_shapes=scratch + [pltpu.VMEM((FFT_R, FFT_N2, ct), F32)],
        compiler_params=_cparams(("parallel", "arbitrary"), 56),
        name="long_conv",
    )(z, hf, m1, tf, ti, m1i, spare)


def _ctx_conv_kernel(z_ref, filt_ref, fwd_ref, inv_ref, _, y_ref):
    half = fwd_ref.shape[0] // 2
    xf = jnp.dot(fwd_ref[...], filt_ref[...], preferred_element_type=F32)
    hr = xf[0:half, 0:D] + xf[0:half, D:2 * D]
    hi = xf[half:, 0:D] - xf[half:, D:2 * D]
    x = jnp.dot(fwd_ref[...], z_ref[...], preferred_element_type=F32)
    xr, xi = x[0:half], x[half:]
    p = jnp.concatenate([xr * hr - xi * hi, xr * hi + xi * hr], axis=0).astype(BF16)
    y_ref[...] = jnp.dot(inv_ref[...], p, preferred_element_type=F32).astype(y_ref.dtype)


def _hyena_long_conv(z, filt_lat, filt_ctx, spare):
    y = _long_conv_latent(z, filt_lat, spare)
    fwd, inv = _direct_dft(CTX)
    full = lambda shape: pl.BlockSpec(shape, lambda b: (0,) * len(shape))
    ctx_rows = pl.BlockSpec((CTX, D), lambda b: (T_LAT // CTX + b, 0))
    return pl.pallas_call(
        _ctx_conv_kernel,
        grid=(B,),
        in_specs=[ctx_rows, full(filt_ctx.shape), full(fwd.shape), full(inv.shape),
                  pl.BlockSpec(memory_space=pl.ANY)],
        out_specs=ctx_rows,
        out_shape=jax.ShapeDtypeStruct((T, D), BF16),
        input_output_aliases={4: 0},
        compiler_params=_cparams(("parallel",), 48),
        name="ctx_conv",
    )(z, filt_ctx, fwd, inv, y)


def _layer_tail_kernel(*refs, hyena):
    h3, y3, refs = refs[0:3], refs[3:6], refs[6:]
    if hyena:
        z3, x03, skip_ref, refs = refs[0:3], refs[3:6], refs[6], refs[7:]
    mod_ref, g_ref, wo_ref, bo_ref, wup_ref, cw_ref, cb_ref, wd_ref, o_ref, xn_ref, gate_ref = refs
    i = pl.program_id(0)
    m = mod_ref[...]
    half = TM // 2
    ext_half = half + HALO

    def ext_rows(trio, s):
        prev_ref, main_ref, next_ref = trio[1], trio[0], trio[2]
        if s == 0:
            return jnp.concatenate([prev_ref[...], main_ref[0:half, :]], axis=0)
        return jnp.concatenate([main_ref[half:, :], next_ref[...]], axis=0)

    for s in range(2):
        mix = ext_rows(y3, s)
        if hyena:
            mix = ((mix.astype(F32) + ext_rows(z3, s).astype(F32) * skip_ref[...])
                   * ext_rows(x03, s).astype(F32)).astype(BF16)
        r = jnp.dot(mix, wo_ref[...], preferred_element_type=F32) + bo_ref[...]
        h_mid = ext_rows(h3, s) + m[:, 2 * D:3 * D] * _rms(r, g_ref[1:2, :])
        xn_ref[s * ext_half:(s + 1) * ext_half, :] = _norm_mod(
            h_mid, g_ref[2:3, :], m[:, 3 * D:4 * D], m[:, 4 * D:5 * D]).astype(BF16)
        o_ref[s * half:(s + 1) * half, :] = h_mid[HALO:, :] if s == 0 else h_mid[0:half, :]

    has_prev, has_next = _seg_masks(i)
    ext = SUB_ROWS + 2 * HALO
    n_sb = TM // SUB_ROWS
    units = [(c0, fc, sb) for c0, fc in FC_CHUNKS for sb in range(n_sb)]

    def up_matmuls(k):
        c0, fc, sb = units[k]
        r0 = sb * SUB_ROWS
        gate_ref[k % 2, :, 0:fc] = jnp.dot(xn_ref[r0:r0 + ext, :], wup_ref[:, c0:c0 + fc],
                                           preferred_element_type=F32)
        return jnp.dot(xn_ref[HALO + r0:HALO + r0 + SUB_ROWS, :],
                       wup_ref[:, D_FF + c0:D_FF + c0 + fc], preferred_element_type=F32)

    acc = [None] * n_sb
    up_next = up_matmuls(0)
    for k, (c0, fc, sb) in enumerate(units):
        up = up_next
        if k + 1 < len(units):
            up_next = up_matmuls(k + 1)
        cols = slice(c0, c0 + fc)
        rows = slice(sb * SUB_ROWS, (sb + 1) * SUB_ROWS)
        conv = _conv3(gate_ref.at[k % 2, :, 0:fc], cw_ref[:, cols], cb_ref[:, cols], has_prev[rows], has_next[rows],
                      SUB_ROWS)
        part = jnp.dot((_silu(conv) * up).astype(BF16), wd_ref[cols, :], preferred_element_type=F32)
        acc[sb] = part if c0 == 0 else acc[sb] + part
    for sb in range(n_sb):
        rows = slice(sb * SUB_ROWS, (sb + 1) * SUB_ROWS)
        o_ref[rows, :] = o_ref[rows, :] + m[:, 5 * D:6 * D] * _rms(acc[sb], g_ref[3:4, :])


def _layer_tail(h, y, mods_l, g, w_out, b_out, layer, w_up, cw, cb, w_down, n_tiles, hyena_args=None):
    resident = lambda shape: pl.BlockSpec(shape, lambda i: (0,) * len(shape), pipeline_mode=pl.Buffered(1))
    stacked = lambda shape: pl.BlockSpec((None,) + shape, lambda i: (layer,) + (0,) * len(shape),
                                         pipeline_mode=pl.Buffered(1))
    hyena = hyena_args is not None
    in_specs = _row_specs(h.shape[0]) + _row_specs(y.shape[0])
    args = [h, h, h, y, y, y]
    if hyena:
        z, x0, skip = hyena_args
        in_specs += _row_specs(z.shape[0]) + _row_specs(x0.shape[0]) + [resident((1, D))]
        args += [z, z, z, x0, x0, x0, skip]
    in_specs += [
        pl.BlockSpec((None, 1, N_MOD * D), lambda i: (_mod_row(i), 0, 0)),
        pl.BlockSpec((4, D), lambda i: (0, 0)),
        resident((D, D)), resident((1, D)),
        stacked((D, 2 * D_FF)), stacked((3, D_FF)), stacked((1, D_FF)), stacked((D_FF, D)),
    ]
    args += [mods_l, g, w_out, b_out, w_up, cw, cb.reshape(DEPTH, 1, D_FF), w_down]
    return pl.pallas_call(
        functools.partial(_layer_tail_kernel, hyena=hyena),
        grid=(n_tiles,),
        in_specs=in_specs,
        out_specs=pl.BlockSpec((TM, D), lambda i: (i, 0)),
        out_shape=jax.ShapeDtypeStruct((n_tiles * TM, D), F32),
        scratch_shapes=[
            pltpu.VMEM((TM + 2 * HALO, D), BF16),
            pltpu.VMEM((2, SUB_ROWS + 2 * HALO, FC), F32),
        ],
        compiler_params=_cparams(("parallel",), 56),
        name="layer_tail",
    )(*args)


def _dup_heads(w):
    k0, k1 = w[..., 0:C_DH], w[..., C_DH:2 * C_DH]
    return jnp.concatenate([k0, k0, k1, k1], axis=-1)


def kernel(x, c, ctx, c_ctx, mod_w, mod_b, norm_g, ffn_w_up, ffn_conv_w, ffn_conv_b, ffn_w_down, a_w_qkv, a_q_gain, a_k_gain, a_w_out, b_w_in, b_b_in, b_conv_w, b_conv_b, b_f_w1, b_f_b1, b_f_w2, b_f_b2, b_f_w3, b_f_b3, b_f_w4, b_skip, b_w_out, b_b_out, c_w_qkv, c_b_qkv, c_sink, c_w_out, c_b_out):
    h = (x.reshape(T_LAT, D), ctx.reshape(T_CTX, D))
    cs = jnp.concatenate([c, c_ctx[None], jnp.zeros((8 - B - 1, D), F32)], axis=0)
    mods = _mods(cs, mod_w, mod_b).reshape(DEPTH, 8, 1, N_MOD * D)
    cos_a, sin_a = _rope_tables(A_DH)
    cos_c, sin_c = _rope_tables(C_DH)
    zeros_d = jnp.zeros((1, D), F32)
    w_up_bf16, w_down_bf16 = ffn_w_up.astype(BF16), ffn_w_down.astype(BF16)
    y = None

    for i in range(DEPTH):
        last = i == DEPTH - 1
        kind, j = i % 3, i // 3
        g = norm_g[i]
        mods_l = mods[i]
        n_tiles = N_LAT_TILES if last else N_TILES
        if kind == 0:
            qkv = _proj_attn(h, mods_l, g, a_w_qkv[j].astype(BF16), jnp.zeros((1, QKV_COLS), F32),
                             a_q_gain[j].reshape(1, A_DH), a_k_gain[j].reshape(1, A_DH), cos_a, sin_a, "a")
            if isinstance(h, tuple):
                qkv, h = qkv
            y = _attn_a(qkv, with_ctx=not last)
            w_out, b_out, hyena_args = a_w_out[j], zeros_d, None
        elif kind == 1:
            x0, z = _proj_b(h, mods_l, g, b_w_in[j].astype(BF16), b_b_in[j].reshape(1, 3 * D),
                            b_conv_w[j], b_conv_b[j].reshape(1, 3 * D))
            fw = (b_f_w1[j], b_f_b1[j], b_f_w2[j], b_f_b2[j], b_f_w3[j], b_f_b3[j], b_f_w4[j])
            spare = y if y is not None and y.shape == (T, D) else jnp.zeros((T, D), BF16)
            y = _hyena_long_conv(z, _hyena_filter(L, *fw), _hyena_filter(CTX, *fw), spare)
            w_out, b_out = b_w_out[j], b_b_out[j].reshape(1, D)
            hyena_args = (z, x0, b_skip[j].reshape(1, D))
        else:
            qc = C_HEADS * C_DH
            kc = qc + C_KV * C_DH
            w, bias = c_w_qkv[j], c_b_qkv[j].reshape(1, -1)
            w = jnp.concatenate([w[:, :qc], _dup_heads(w[:, qc:kc]), _dup_heads(w[:, kc:])], axis=1)
            bias = jnp.concatenate([bias[:, :qc], _dup_heads(bias[:, qc:kc]), _dup_heads(bias[:, kc:])], axis=1)
            ones = jnp.ones((1, 128), F32)
            qkv = _proj_attn(h, mods_l, g, w.astype(BF16), bias, ones, ones, cos_c, sin_c, "c")
            sink_rows = jnp.broadcast_to((c_sink[j].astype(F32) * LOG2_E)[:, None, None],
                                         (C_HEADS, 8, 128)).reshape(C_KV, (C_HEADS // C_KV) * 8, 128)
            y = _attn_c(qkv, sink_rows)
            w_out, b_out, hyena_args = c_w_out[j], c_b_out[j].reshape(1, D), None
        h = _layer_tail(h, y, mods_l, g, w_out.astype(BF16), b_out, i, w_up_bf16, ffn_conv_w, ffn_conv_b,
                        w_down_bf16, n_tiles, hyena_args)
    return h.reshape(B, L, D)
```
